```python
import math
import jax, jax.numpy as jnp
from jax import lax
import numpy as np

D_MODEL = 4096
BATCH = 2
SEQ = 8192
DEPTH = 4

MIX_WIDTH = D_MODEL
MLSTM_WIDTH = MIX_WIDTH // 2
SGU_WIDTH = MIX_WIDTH - MLSTM_WIDTH
MLSTM_HEADS = 8
MLSTM_HEAD_DIM = MLSTM_WIDTH // MLSTM_HEADS
MLSTM_CHUNK = 128
CONV_WIDTH = 3
F_BIAS_LO, F_BIAS_HI = 3.0, 6.0
N_GATES = 4 * MLSTM_HEADS
SGU_GROUPS = 8
SGU_GROUP_DIM = SGU_WIDTH // SGU_GROUPS
SGU_CHUNK = 128
OFF_Q = 0
OFF_K = OFF_Q + MLSTM_WIDTH
OFF_V = OFF_K + MLSTM_WIDTH
OFF_O = OFF_V + MLSTM_WIDTH
OFF_G = OFF_O + MLSTM_WIDTH
OFF_U = OFF_G + N_GATES
OFF_SV = OFF_U + SGU_WIDTH
P_IN = OFF_SV + SGU_WIDTH
N_EXPERTS = 64
TOP_K = 8
D_EXPERT = 128
D_SHARED = 1024
ROUTE_SCALE = 2.5
DEEPNORM_ALPHA = (2 * DEPTH) ** 0.25
DEEPNORM_BETA = (8 * DEPTH) ** -0.25
LN_EPS = 1e-5

kernel_name = "hybrid_mlstm_sgu_moe_deepnorm_encoder"


def layer_norm(x, g, b):
    xf = x.astype(jnp.float32)
    mu = jnp.mean(xf, axis=-1, keepdims=True)
    var = jnp.mean(jnp.square(xf - mu), axis=-1, keepdims=True)
    return ((xf - mu) * lax.rsqrt(var + LN_EPS) * g + b).astype(x.dtype)


def centered_dwconv(z, w):
    c = z.shape[-1]
    return lax.conv_general_dilated(
        z, w[:, None, :].astype(z.dtype), window_strides=(1,), padding="SAME",
        dimension_numbers=("NWC", "WIO", "NWC"), feature_group_count=c)


def mlstm_chunkwise(q, k, v, ig, lf):
    bsz, nh, s, dh = q.shape
    L = MLSTM_CHUNK
    nc = s // L
    to_chunks = lambda a: jnp.moveaxis(a.reshape(bsz, nh, nc, L, *a.shape[3:]), 2, 0)
    tril = jnp.tril(jnp.ones((L, L), dtype=bool))

    def step(carry, inp):
        C, n, m = carry
        qc, kc, vc, ic, fc = inp
        b = jnp.cumsum(fc, axis=-1)
        dmat = jnp.where(tril, b[..., :, None] - b[..., None, :] + ic[..., None, :], -jnp.inf)
        inter = b + m[..., None]
        m_t = jnp.maximum(inter, jnp.max(dmat, axis=-1))
        w_intra = jnp.exp(dmat - m_t[..., None])
        w_inter = jnp.exp(inter - m_t)
        s_qk = jnp.einsum('bhtd,bhsd->bhts', qc, kc) * w_intra
        num = jnp.einsum('bhts,bhsd->bhtd', s_qk, vc) + w_inter[..., None] * jnp.einsum('bhvk,bhtk->bhtv', C, qc)
        den = jnp.sum(s_qk, axis=-1) + w_inter * jnp.einsum('bhk,bhtk->bht', n, qc)
        h = num / jnp.maximum(jnp.abs(den), jnp.exp(-m_t))[..., None]
        g = b[..., -1]
        a = g[..., None] - b + ic
        m_new = jnp.maximum(g + m, jnp.max(a, axis=-1))
        wa = jnp.exp(a - m_new[..., None])
        decay = jnp.exp(g + m - m_new)
        C_new = decay[..., None, None] * C + jnp.einsum('bhs,bhsv,bhsk->bhvk', wa, vc, kc)
        n_new = decay[..., None] * n + jnp.einsum('bhs,bhsk->bhk', wa, kc)
        return (C_new, n_new, m_new), h

    init = (jnp.zeros((bsz, nh, dh, dh), jnp.float32),
            jnp.zeros((bsz, nh, dh), jnp.float32),
            jnp.zeros((bsz, nh), jnp.float32))
    _, hs = lax.scan(step, init, (to_chunks(q), to_chunks(k), to_chunks(v), to_chunks(ig), to_chunks(lf)))
    return jnp.moveaxis(hs, 0, 2).reshape(bsz, nh, s, dh)


def mlstm_group(zq, zk, zv, zo, gates, conv_w, head_g):
    bsz, s, _ = zq.shape
    H, Dh = MLSTM_HEADS, MLSTM_HEAD_DIM
    qk = jax.nn.silu(centered_dwconv(jnp.concatenate([zq, zk], axis=-1), conv_w))
    heads = lambda a: a.astype(jnp.float32).reshape(bsz, s, H, Dh).transpose(0, 2, 1, 3)
    q = heads(qk[..., :MLSTM_WIDTH])
    k = heads(qk[..., MLSTM_WIDTH:]) * (Dh ** -0.5)
    v = heads(zv)
    g = gates.transpose(0, 2, 1)
    i_f, f_f, i_b, f_b = (g[:, j * H:(j + 1) * H] for j in range(4))
    h_fwd = mlstm_chunkwise(q, k, v, i_f, jax.nn.log_sigmoid(f_f))
    flip = lambda a: jnp.flip(a, axis=2)
    h_bwd = flip(mlstm_chunkwise(flip(q), flip(k), flip(v), flip(i_b), flip(jax.nn.log_sigmoid(f_b))))
    h = (h_fwd + h_bwd).transpose(0, 2, 1, 3)
    h = jax.nn.sigmoid(zo.astype(jnp.float32)).reshape(bsz, s, H, Dh) * h
    mu = jnp.mean(h, axis=-1, keepdims=True)
    var = jnp.mean(jnp.square(h - mu), axis=-1, keepdims=True)
    h = (h - mu) * lax.rsqrt(var + LN_EPS) * head_g.reshape(H, Dh)
    return h.reshape(bsz, s, MLSTM_WIDTH).astype(zq.dtype)


def spatial_gating_group(zu, zv, norm_g, norm_b, w_s, b_s):
    bsz, s, _ = zu.shape
    G, Dg, CH = SGU_GROUPS, SGU_GROUP_DIM, SGU_CHUNK
    u = jax.nn.gelu(zu)
    v = jax.nn.gelu(zv).reshape(bsz, s, G, Dg)
    v = layer_norm(v, norm_g.reshape(G, Dg), norm_b.reshape(G, Dg))
    v = v.reshape(bsz, s // CH, CH, G, Dg)
    sp = jnp.einsum('gts,bcsgd->bctgd', w_s, v) + b_s.T[:, :, None]
    return u * sp.reshape(bsz, s, SGU_WIDTH).astype(zu.dtype)


def moe_ffn(x, w_r, b_r, wg, wu, wd, sg, su, sd):
    bsz, s, d = x.shape
    t = x.reshape(-1, d)
    scores = jax.nn.sigmoid((t @ w_r).astype(jnp.float32))
    _, idx = lax.top_k(scores + b_r, TOP_K)
    w = jnp.take_along_axis(scores, idx, axis=-1)
    w = w / jnp.sum(w, axis=-1, keepdims=True) * ROUTE_SCALE
    comb = jnp.zeros_like(scores).at[jnp.arange(t.shape[0])[:, None], idx].set(w)
    h = jax.nn.silu(jnp.einsum('td,edf->tef', t, wg)) * jnp.einsum('td,edf->tef', t, wu)
    routed = jnp.einsum('tef,efd->td', h * comb[:, :, None].astype(h.dtype), wd)
    shared = (jax.nn.silu(t @ sg) * (t @ su)) @ sd
    return (routed + shared).reshape(bsz, s, d)


def setup_inputs(seed: int = 0) -> dict:
    key = jax.random.key(seed)
    ks = jax.random.split(key, 24)
    nrm = lambda k, shape, scale: jax.random.normal(k, shape, jnp.float32) * scale
    L, D, H = DEPTH, D_MODEL, MLSTM_HEADS
    f_bias = jnp.linspace(F_BIAS_LO, F_BIAS_HI, H)
    gate_base = jnp.concatenate([jnp.zeros((H,)), f_bias, jnp.zeros((H,)), f_bias])
    return {
        "x": nrm(ks[0], (BATCH, SEQ, D), 1.0),
        "w_in": nrm(ks[1], (L, D, P_IN), D ** -0.5),
        "b_gates": gate_base[None, :] + nrm(ks[2], (L, N_GATES), 0.1),
        "conv_qk": nrm(ks[3], (L, CONV_WIDTH, 2 * MLSTM_WIDTH), CONV_WIDTH ** -0.5),
        "mlstm_norm_g": 1.0 + nrm(ks[4], (L, MLSTM_WIDTH), 0.02),
        "sgu_norm_g": 1.0 + nrm(ks[5], (L, SGU_WIDTH), 0.02),
        "sgu_norm_b": nrm(ks[6], (L, SGU_WIDTH), 0.02),
        "w_spatial": nrm(ks[7], (L, SGU_GROUPS, SGU_CHUNK, SGU_CHUNK), SGU_CHUNK ** -0.5),
        "b_spatial": 1.0 + nrm(ks[8], (L, SGU_GROUPS, SGU_CHUNK), 0.1),
        "w_out": nrm(ks[9], (L, MIX_WIDTH, D), DEEPNORM_BETA * MIX_WIDTH ** -0.5),
        "ln1_g": 1.0 + nrm(ks[10], (L, D), 0.02),
        "ln1_b": nrm(ks[11], (L, D), 0.02),
        "w_router": nrm(ks[12], (L, D, N_EXPERTS), D ** -0.5),
        "b_router": nrm(ks[13], (L, N_EXPERTS), 0.01),
        "w_exp_gate": nrm(ks[14], (L, N_EXPERTS, D, D_EXPERT), D ** -0.5),
        "w_exp_up": nrm(ks[15], (L, N_EXPERTS, D, D_EXPERT), D ** -0.5),
        "w_exp_down": nrm(ks[16], (L, N_EXPERTS, D_EXPERT, D), DEEPNORM_BETA * D_EXPERT ** -0.5),
        "w_sh_gate": nrm(ks[17], (L, D, D_SHARED), D ** -0.5),
        "w_sh_up": nrm(ks[18], (L, D, D_SHARED), D ** -0.5),
        "w_sh_down": nrm(ks[19], (L, D_SHARED, D), DEEPNORM_BETA * D_SHARED ** -0.5),
        "ln2_g": 1.0 + nrm(ks[20], (L, D), 0.02),
        "ln2_b": nrm(ks[21], (L, D), 0.02),
    }


def reference(x, w_in, b_gates, conv_qk, mlstm_norm_g, sgu_norm_g, sgu_norm_b, w_spatial, b_spatial,
              w_out, ln1_g, ln1_b, w_router, b_router, w_exp_gate, w_exp_up, w_exp_down,
              w_sh_gate, w_sh_up, w_sh_down, ln2_g, ln2_b):
    for l in range(DEPTH):
        z = x @ w_in[l]
        gates = z[..., OFF_G:OFF_U].astype(jnp.float32) + b_gates[l]
        h_m = mlstm_group(z[..., OFF_Q:OFF_K], z[..., OFF_K:OFF_V], z[..., OFF_V:OFF_O],
                          z[..., OFF_O:OFF_G], gates, conv_qk[l], mlstm_norm_g[l])
        h_s = spatial_gating_group(z[..., OFF_U:OFF_SV], z[..., OFF_SV:P_IN], sgu_norm_g[l],
                                   sgu_norm_b[l], w_spatial[l], b_spatial[l])
        mix = jnp.concatenate([h_m, h_s], axis=-1) @ w_out[l]
        x = layer_norm(DEEPNORM_ALPHA * x + mix, ln1_g[l], ln1_b[l])
        ffn = moe_ffn(x, w_router[l], b_router[l], w_exp_gate[l], w_exp_up[l], w_exp_down[l],
                      w_sh_gate[l], w_sh_up[l], w_sh_down[l])
        x = layer_norm(DEEPNORM_ALPHA * x + ffn, ln2_g[l], ln2_b[l])
    return x
```

```python
import functools
import math

import jax
import jax.numpy as jnp
from jax import lax
from jax.experimental import pallas as pl
from jax.experimental.pallas import tpu as pltpu

MLSTM_HEADS = 8
MLSTM_HEAD_DIM = 256
CHUNK = 128
SGU_GROUPS = 8
SGU_GROUP_DIM = 256
N_GATES = 4 * MLSTM_HEADS
N_EXPERTS = 64
TOP_K = 8
D_EXPERT = 128
ROUTE_SCALE = 2.5
DEPTH_FOR_NORM = 4
DEEPNORM_ALPHA = (2 * DEPTH_FOR_NORM) ** 0.25
LN_EPS = 1e-5

V7X_LANES = 128
V7X_VMEM_LIMIT_BYTES = 60000 * 1024

F32 = jnp.float32
BF16 = jnp.bfloat16


def _cparams(n_axes, vmem_bytes):
    limit = int(min(max(vmem_bytes * 5 // 4 + (4 << 20), 16 << 20), V7X_VMEM_LIMIT_BYTES))
    return pltpu.CompilerParams(dimension_semantics=("arbitrary",) * n_axes, vmem_limit_bytes=limit)


def _nbytes(shape, dtype):
    return math.prod(shape) * jnp.dtype(dtype).itemsize


def _mm_kernel(a_ref, b_ref, o_ref):
    o_ref[...] = jnp.dot(a_ref[...], b_ref[...], preferred_element_type=F32).astype(o_ref.dtype)


def matmul(a, b, *, tm, tn, out_dtype, name):
    m, k = a.shape
    _, n = b.shape
    tm, tn = min(tm, m), min(tn, n)
    assert m % tm == 0 and n % tn == 0
    vmem = 2 * (_nbytes((tm, k), a.dtype) + _nbytes((k, tn), b.dtype) + _nbytes((tm, tn), out_dtype))
    vmem += _nbytes((tm, tn), F32)
    return pl.pallas_call(
        _mm_kernel,
        grid=(n // tn, m // tm),
        in_specs=[pl.BlockSpec((tm, k), lambda j, i: (i, 0)),
                  pl.BlockSpec((k, tn), lambda j, i: (0, j))],
        out_specs=pl.BlockSpec((tm, tn), lambda j, i: (i, j)),
        out_shape=jax.ShapeDtypeStruct((m, n), out_dtype),
        compiler_params=_cparams(2, vmem),
        name=name,
    )(a, b)


def _mm2_kernel(a1_ref, a2_ref, b_ref, o_ref):
    k1 = a1_ref.shape[1]
    acc = jnp.dot(a1_ref[...], b_ref[:k1, :], preferred_element_type=F32)
    acc = acc + jnp.dot(a2_ref[...], b_ref[k1:, :], preferred_element_type=F32)
    o_ref[...] = acc.astype(o_ref.dtype)


def matmul_cat(a1, a2, b, *, tm, tn, out_dtype, name):
    m, k1 = a1.shape
    _, k2 = a2.shape
    k, n = b.shape
    assert k == k1 + k2
    tm, tn = min(tm, m), min(tn, n)
    assert m % tm == 0 and n % tn == 0
    vmem = 2 * (_nbytes((tm, k), a1.dtype) + _nbytes((k, tn), b.dtype) + _nbytes((tm, tn), out_dtype))
    vmem += _nbytes((tm, tn), F32)
    return pl.pallas_call(
        _mm2_kernel,
        grid=(n // tn, m // tm),
        in_specs=[pl.BlockSpec((tm, k1), lambda j, i: (i, 0)),
                  pl.BlockSpec((tm, k2), lambda j, i: (i, 0)),
                  pl.BlockSpec((k, tn), lambda j, i: (0, j))],
        out_specs=pl.BlockSpec((tm, tn), lambda j, i: (i, j)),
        out_shape=jax.ShapeDtypeStruct((m, n), out_dtype),
        compiler_params=_cparams(2, vmem),
        name=name,
    )(a1, a2, b)


def _add_ln_kernel(x_ref, r_ref, g_ref, b_ref, y_ref, yb_ref):
    v = DEEPNORM_ALPHA * x_ref[...] + r_ref[...]
    mu = jnp.mean(v, axis=-1, keepdims=True)
    c = v - mu
    var = jnp.mean(c * c, axis=-1, keepdims=True)
    y = c * lax.rsqrt(var + LN_EPS) * g_ref[...] + b_ref[...]
    y_ref[...] = y
    yb_ref[...] = y.astype(BF16)


def add_layer_norm(x, r, g, b, *, tm, name):
    t, d = x.shape
    tm = min(tm, t)
    assert t % tm == 0
    vmem = 2 * (3 * _nbytes((tm, d), F32) + _nbytes((tm, d), BF16)) + 4 * _nbytes((tm, d), F32)
    row = lambda i: (i, 0)
    return pl.pallas_call(
        _add_ln_kernel,
        grid=(t // tm,),
        in_specs=[pl.BlockSpec((tm, d), row), pl.BlockSpec((tm, d), row),
                  pl.BlockSpec((1, d), lambda i: (0, 0)), pl.BlockSpec((1, d), lambda i: (0, 0))],
        out_specs=[pl.BlockSpec((tm, d), row), pl.BlockSpec((tm, d), row)],
        out_shape=[jax.ShapeDtypeStruct((t, d), F32), jax.ShapeDtypeStruct((t, d), BF16)],
        compiler_params=_cparams(1, vmem),
        name=name,
    )(x, r, g.reshape(1, d), b.reshape(1, d))


_HALO = 16


def _conv_silu_kernel(z_ref, zp_ref, zn_ref, w_ref, s_ref, o_ref, *, seq_len):
    tr = z_ref.shape[0]
    i = pl.program_id(0)
    x = z_ref[...].astype(F32)
    row = lax.broadcasted_iota(jnp.int32, x.shape, 0)
    t0 = i * tr
    prev_row = jnp.where(lax.rem(t0, seq_len) == 0, 0.0, zp_ref[_HALO - 1:_HALO, :].astype(F32))
    next_row = jnp.where(lax.rem(t0 + tr, seq_len) == 0, 0.0, zn_ref[0:1, :].astype(F32))
    x_prev = jnp.where(row == 0, prev_row, pltpu.roll(x, 1, axis=0))
    x_next = jnp.where(row == tr - 1, next_row, pltpu.roll(x, tr - 1, axis=0))
    y = w_ref[0:1, :] * x_prev + w_ref[1:2, :] * x + w_ref[2:3, :] * x_next
    o_ref[...] = (y * jax.nn.sigmoid(y) * s_ref[...]).astype(o_ref.dtype)


def conv_silu(z, conv_w, col_scale, *, seq_len, tr, tc, name):
    t = z.shape[0]
    c = conv_w.shape[1]
    tr = min(tr, seq_len)
    assert seq_len % tr == 0 and tr % _HALO == 0 and c % tc == 0
    hb = tr // _HALO
    n_hb = t // _HALO
    vmem = 2 * (2 * _nbytes((tr, tc), BF16) + 2 * _nbytes((_HALO, tc), BF16)) + 6 * _nbytes((tr, tc), F32)
    return pl.pallas_call(
        functools.partial(_conv_silu_kernel, seq_len=seq_len),
        grid=(t // tr, c // tc),
        in_specs=[pl.BlockSpec((tr, tc), lambda i, j: (i, j)),
                  pl.BlockSpec((_HALO, tc), lambda i, j: (jnp.maximum(i * hb - 1, 0), j)),
                  pl.BlockSpec((_HALO, tc), lambda i, j: (jnp.minimum((i + 1) * hb, n_hb - 1), j)),
                  pl.BlockSpec((3, tc), lambda i, j: (0, j)),
                  pl.BlockSpec((1, tc), lambda i, j: (0, j))],
        out_specs=pl.BlockSpec((tr, tc), lambda i, j: (i, j)),
        out_shape=jax.ShapeDtypeStruct((t, c), BF16),
        compiler_params=_cparams(2, vmem),
        name=name,
    )(z, z, z, conv_w, col_scale)


def _chunk_scan(x, lane_in_chunk, *, suffix):
    n = x.shape[-1]
    k = 1
    while k < CHUNK:
        if suffix:
            shifted = pltpu.roll(x, n - k, axis=1)
            x = x + jnp.where(lane_in_chunk < CHUNK - k, shifted, 0.0)
        else:
            shifted = pltpu.roll(x, k, axis=1)
            x = x + jnp.where(lane_in_chunk >= k, shifted, 0.0)
        k *= 2
    return x


def _gate_prep_kernel(g_ref, bias_ref, o_ref):
    i_f = g_ref[0] + bias_ref[0]
    lf_f = jax.nn.log_sigmoid(g_ref[1] + bias_ref[1])
    i_b = g_ref[2] + bias_ref[2]
    lf_b = jax.nn.log_sigmoid(g_ref[3] + bias_ref[3])
    lane = lax.rem(lax.broadcasted_iota(jnp.int32, i_f.shape, 1), CHUNK)
    b_f = _chunk_scan(lf_f, lane, suffix=False)
    g_f = b_f + _chunk_scan(lf_f, lane, suffix=True) - lf_f
    b_b = _chunk_scan(lf_b, lane, suffix=True)
    g_b = b_b + _chunk_scan(lf_b, lane, suffix=False) - lf_b
    r_f = i_f - b_f
    r_b = i_b - b_b
    o_ref[0] = b_f
    o_ref[1] = r_f
    o_ref[2] = g_f + r_f
    o_ref[3] = g_f
    o_ref[4] = b_b
    o_ref[5] = r_b
    o_ref[6] = g_b + r_b
    o_ref[7] = g_b


def gate_prep(g4, bias, *, tl, name):
    _, h, t = g4.shape
    tl = min(tl, t)
    assert t % tl == 0 and tl % CHUNK == 0
    vmem = 2 * (_nbytes((4, h, tl), F32) + _nbytes((8, h, tl), F32)) + 16 * _nbytes((h, tl), F32)
    return pl.pallas_call(
        _gate_prep_kernel,
        grid=(t // tl,),
        in_specs=[pl.BlockSpec((4, h, tl), lambda i: (0, 0, i)),
                  pl.BlockSpec((4, h, 1), lambda i: (0, 0, 0))],
        out_specs=pl.BlockSpec((8, h, tl), lambda i: (0, 0, i)),
        out_shape=jax.ShapeDtypeStruct((8, h, t), F32),
        compiler_params=_cparams(1, vmem),
        name=name,
    )(g4, bias)


_NT = (((1,), (1,)), ((), ()))
_TN = (((0,), (0,)), ((), ()))


def _mlstm_chunk(q, k, v, b_col, a_col, g11, r_row, ct_ref, n_ref, m11, mask):
    dmat = jnp.where(mask, b_col + r_row, -jnp.inf)
    inter = b_col + m11
    m_t = jnp.maximum(inter, jnp.max(dmat, axis=1, keepdims=True))
    w_intra = jnp.exp(dmat - m_t)
    w_inter = jnp.exp(inter - m_t)
    s_qk = lax.dot_general(q, k, _NT, preferred_element_type=F32) * w_intra
    ct = ct_ref[...]
    num = jnp.dot(s_qk.astype(BF16), v, preferred_element_type=F32)
    num = num + w_inter * jnp.dot(q, ct.astype(BF16), preferred_element_type=F32)
    qn = jnp.sum(q.astype(F32) * n_ref[...], axis=1, keepdims=True)
    den = jnp.sum(s_qk, axis=1, keepdims=True) + w_inter * qn
    h = num / jnp.maximum(jnp.abs(den), jnp.exp(-m_t))
    m_new = jnp.maximum(g11 + m11, jnp.max(a_col, axis=0, keepdims=True))
    wa = jnp.exp(a_col - m_new)
    decay = jnp.exp(g11 + m11 - m_new)
    kw = k.astype(F32) * wa
    ct_ref[...] = decay * ct + lax.dot_general(kw.astype(BF16), v, _TN, preferred_element_type=F32)
    n_ref[...] = decay * n_ref[...] + jnp.sum(kw, axis=0, keepdims=True)
    return h, m_new


def _mlstm_kernel(q_ref, k_ref, v_ref, o_ref, rows_ref, cols_ref, hg_ref, out_ref,
                  acc_ref, ctf_ref, ctb_ref, nf_ref, nb_ref):
    nc = rows_ref.shape[0]
    L = CHUNK
    ctf_ref[...] = jnp.zeros_like(ctf_ref)
    ctb_ref[...] = jnp.zeros_like(ctb_ref)
    nf_ref[...] = jnp.zeros_like(nf_ref)
    nb_ref[...] = jnp.zeros_like(nb_ref)
    t_idx = lax.broadcasted_iota(jnp.int32, (L, L), 0)
    s_idx = lax.broadcasted_iota(jnp.int32, (L, L), 1)
    causal = s_idx <= t_idx
    anti = s_idx >= t_idx

    def directions(c, m_f, m_b):
        cb = nc - 1 - c
        sf = pl.ds(pl.multiple_of(c * L, L), L)
        sb = pl.ds(pl.multiple_of(cb * L, L), L)
        cols_f = cols_ref[sf, :]
        cols_b = cols_ref[sb, :]
        rows_f = rows_ref[c]
        rows_b = rows_ref[cb]
        h_f, m_f = _mlstm_chunk(q_ref[sf, :], k_ref[sf, :], v_ref[sf, :],
                                cols_f[:, 0:1], cols_f[:, 2:3], cols_f[L - 1:L, 3:4], rows_f[1:2, :],
                                ctf_ref, nf_ref, m_f, causal)
        h_b, m_b = _mlstm_chunk(q_ref[sb, :], k_ref[sb, :], v_ref[sb, :],
                                cols_b[:, 4:5], cols_b[:, 6:7], cols_b[0:1, 7:8], rows_b[5:6, :],
                                ctb_ref, nb_ref, m_b, anti)
        return sf, sb, h_f, h_b, m_f, m_b

    def first_half(c, carry):
        sf, sb, h_f, h_b, m_f, m_b = directions(c, *carry)
        acc_ref[sf, :] = h_f
        acc_ref[sb, :] = h_b
        return m_f, m_b

    def finish(sl, h_new):
        hh = (acc_ref[sl, :] + h_new) * jax.nn.sigmoid(o_ref[sl, :].astype(F32))
        mu = jnp.mean(hh, axis=1, keepdims=True)
        cen = hh - mu
        var = jnp.mean(cen * cen, axis=1, keepdims=True)
        out_ref[sl, :] = (cen * lax.rsqrt(var + LN_EPS) * hg_ref[...]).astype(out_ref.dtype)

    def second_half(c, carry):
        sf, sb, h_f, h_b, m_f, m_b = directions(c, *carry)
        finish(sf, h_f)
        finish(sb, h_b)
        return m_f, m_b

    zero = jnp.zeros((1, 1), F32)
    carry = lax.fori_loop(0, nc // 2, first_half, (zero, zero))
    lax.fori_loop(nc // 2, nc, second_half, carry)


def mlstm(qk, z, rows, cols, head_g, *, batch, seq_len, name):
    t = qk.shape[0]
    nh, dh = MLSTM_HEADS, MLSTM_HEAD_DIM
    nc = seq_len // CHUNK
    assert nc % 2 == 0
    single = dict(pipeline_mode=pl.Buffered(1))
    vmem = 4 * _nbytes((seq_len, dh), BF16) + _nbytes((seq_len, V7X_LANES), F32)
    vmem += 2 * _nbytes((seq_len, dh), BF16) + _nbytes((seq_len, dh), F32) + 2 * _nbytes((nc, 8, CHUNK), F32)
    vmem += 4 * _nbytes((dh, dh), F32)
    return pl.pallas_call(
        _mlstm_kernel,
        grid=(batch, nh),
        in_specs=[pl.BlockSpec((seq_len, dh), lambda b, h: (b, h), **single),
                  pl.BlockSpec((seq_len, dh), lambda b, h: (b, nh + h), **single),
                  pl.BlockSpec((seq_len, dh), lambda b, h: (b, 2 * nh + h), **single),
                  pl.BlockSpec((seq_len, dh), lambda b, h: (b, 3 * nh + h), **single),
                  pl.BlockSpec((None, nc, 8, CHUNK), lambda b, h: (h, b, 0, 0)),
                  pl.BlockSpec((None, None, seq_len, 8), lambda b, h: (b, h, 0, 0), **single),
                  pl.BlockSpec((None, 1, dh), lambda b, h: (h, 0, 0))],
        out_specs=pl.BlockSpec((seq_len, dh), lambda b, h: (b, h)),
        out_shape=jax.ShapeDtypeStruct((t, nh * dh), BF16),
        scratch_shapes=[pltpu.VMEM((seq_len, dh), F32),
                        pltpu.VMEM((dh, dh), F32), pltpu.VMEM((dh, dh), F32),
                        pltpu.VMEM((1, dh), F32), pltpu.VMEM((1, dh), F32)],
        compiler_params=_cparams(2, vmem),
        name=name,
    )(qk, qk, z, z, rows, cols, head_g.reshape(nh, 1, dh))


def _gelu_tanh(x):
    return 0.5 * x * (1.0 + jnp.tanh(math.sqrt(2.0 / math.pi) * (x + 0.044715 * (x * x * x))))


def _sgu_kernel(u_ref, v_ref, ng_ref, nb_ref, ws_ref, bs_ref, o_ref):
    n_chunks = u_ref.shape[0] // CHUNK
    ws = ws_ref[...]
    for c in range(n_chunks):
        sl = slice(c * CHUNK, (c + 1) * CHUNK)
        v = _gelu_tanh(v_ref[sl, :].astype(F32))
        mu = jnp.mean(v, axis=1, keepdims=True)
        cen = v - mu
        var = jnp.mean(cen * cen, axis=1, keepdims=True)
        vn = cen * lax.rsqrt(var + LN_EPS) * ng_ref[...] + nb_ref[...]
        sp = jnp.dot(ws, vn.astype(BF16), preferred_element_type=F32) + bs_ref[...]
        o_ref[sl, :] = (_gelu_tanh(u_ref[sl, :].astype(F32)) * sp).astype(o_ref.dtype)


def spatial_gating(z, norm_g, norm_b, w_s, b_s, *, u_col, v_col, tr, name):
    t = z.shape[0]
    g, dg = SGU_GROUPS, SGU_GROUP_DIM
    tr = min(tr, t)
    assert t % tr == 0 and tr % CHUNK == 0
    vmem = 2 * 3 * _nbytes((tr, dg), BF16) + 8 * _nbytes((CHUNK, dg), F32) * (tr // CHUNK)
    return pl.pallas_call(
        _sgu_kernel,
        grid=(g, t // tr),
        in_specs=[pl.BlockSpec((tr, dg), lambda j, i: (i, u_col + j)),
                  pl.BlockSpec((tr, dg), lambda j, i: (i, v_col + j)),
                  pl.BlockSpec((None, 1, dg), lambda j, i: (j, 0, 0)),
                  pl.BlockSpec((None, 1, dg), lambda j, i: (j, 0, 0)),
                  pl.BlockSpec((None, CHUNK, CHUNK), lambda j, i: (j, 0, 0)),
                  pl.BlockSpec((None, CHUNK, 1), lambda j, i: (j, 0, 0))],
        out_specs=pl.BlockSpec((tr, dg), lambda j, i: (i, j)),
        out_shape=jax.ShapeDtypeStruct((t, g * dg), BF16),
        compiler_params=_cparams(2, vmem),
        name=name,
    )(z, z, norm_g.reshape(g, 1, dg), norm_b.reshape(g, 1, dg), w_s.astype(BF16), b_s.reshape(g, CHUNK, 1))


def _router_kernel(x_ref, w_ref, b_ref, comb_ref):
    logits = jnp.dot(x_ref[...], w_ref[...], preferred_element_type=F32)
    lane = lax.broadcasted_iota(jnp.int32, logits.shape, 1)
    valid = lane < N_EXPERTS
    scores = jax.nn.sigmoid(logits)
    sel = jnp.where(valid, scores + b_ref[...], -jnp.inf)
    picked = jnp.zeros(logits.shape, jnp.bool_)
    for _ in range(TOP_K):
        best = jnp.max(sel, axis=1, keepdims=True)
        first = jnp.min(jnp.where(sel == best, lane, V7X_LANES), axis=1, keepdims=True)
        hit = lane == first
        picked = jnp.logical_or(picked, hit)
        sel = jnp.where(hit, -jnp.inf, sel)
    w = jnp.where(picked, scores, 0.0)
    comb_ref[...] = w / jnp.sum(w, axis=1, keepdims=True) * ROUTE_SCALE


def router(xb, w_r, b_r, *, tm, name):
    t, d = xb.shape
    tm = min(tm, t)
    assert t % tm == 0
    vmem = 2 * (_nbytes((tm, d), BF16) + _nbytes((d, V7X_LANES), BF16) + _nbytes((tm, V7X_LANES), F32))
    vmem += 8 * _nbytes((tm, V7X_LANES), F32)
    return pl.pallas_call(
        _router_kernel,
        grid=(t // tm,),
        in_specs=[pl.BlockSpec((tm, d), lambda i: (i, 0)),
                  pl.BlockSpec((d, V7X_LANES), lambda i: (0, 0)),
                  pl.BlockSpec((1, V7X_LANES), lambda i: (0, 0))],
        out_specs=pl.BlockSpec((tm, V7X_LANES), lambda i: (i, 0)),
        out_shape=jax.ShapeDtypeStruct((t, V7X_LANES), F32),
        compiler_params=_cparams(1, vmem),
        name=name,
    )(xb, w_r, b_r)


def _ffn_kernel(x_ref, comb_ref, wg_ref, wu_ref, wd_ref, o_ref, *, n_routed_blocks):
    f = pl.program_id(1)
    tf = wg_ref.shape[1]
    per = tf // D_EXPERT
    x = x_ref[...]
    gate = jnp.dot(x, wg_ref[...], preferred_element_type=F32)
    up = jnp.dot(x, wu_ref[...], preferred_element_type=F32)
    e_row = lax.broadcasted_iota(jnp.int32, (V7X_LANES, tf), 0)
    e_col = lax.broadcasted_iota(jnp.int32, (V7X_LANES, tf), 1) // D_EXPERT + f * per
    expand = jnp.where(jnp.logical_and(e_row == e_col, f < n_routed_blocks), 1.0, 0.0).astype(BF16)
    comb = comb_ref[...]
    c_hi = comb.astype(BF16)
    c_lo = (comb - c_hi.astype(F32)).astype(BF16)
    scale = jnp.dot(c_hi, expand, preferred_element_type=F32) + jnp.dot(c_lo, expand, preferred_element_type=F32)
    scale = scale + jnp.where(f < n_routed_blocks, 0.0, 1.0)
    h = (gate * jax.nn.sigmoid(gate) * up * scale).astype(BF16)
    part = jnp.dot(h, wd_ref[...], preferred_element_type=F32)

    @pl.when(f == 0)
    def _():
        o_ref[...] = part

    @pl.when(f > 0)
    def _():
        o_ref[...] += part


def ffn(xb, comb, wg, wu, wd, *, tm, tf, name):
    t, d = xb.shape
    f_all = wg.shape[1]
    tm = min(tm, t)
    assert t % tm == 0 and f_all % tf == 0 and (N_EXPERTS * D_EXPERT) % tf == 0 and tf % D_EXPERT == 0
    vmem = 2 * (_nbytes((tm, d), BF16) + _nbytes((tm, V7X_LANES), F32) + 3 * _nbytes((d, tf), BF16))
    vmem += 2 * _nbytes((tm, d), F32) + _nbytes((tm, d), F32) + 6 * _nbytes((tm, tf), F32)
    return pl.pallas_call(
        functools.partial(_ffn_kernel, n_routed_blocks=N_EXPERTS * D_EXPERT // tf),
        grid=(t // tm, f_all // tf),
        in_specs=[pl.BlockSpec((tm, d), lambda i, f: (i, 0)),
                  pl.BlockSpec((tm, V7X_LANES), lambda i, f: (i, 0)),
                  pl.BlockSpec((d, tf), lambda i, f: (0, f)),
                  pl.BlockSpec((d, tf), lambda i, f: (0, f)),
                  pl.BlockSpec((tf, d), lambda i, f: (f, 0))],
        out_specs=pl.BlockSpec((tm, d), lambda i, f: (i, 0)),
        out_shape=jax.ShapeDtypeStruct((t, d), F32),
        compiler_params=_cparams(2, vmem),
        name=name,
    )(xb, comb, wg, wu, wd)


def kernel(x, w_in, b_gates, conv_qk, mlstm_norm_g, sgu_norm_g, sgu_norm_b, w_spatial, b_spatial, w_out,
           ln1_g, ln1_b, w_router, b_router, w_exp_gate, w_exp_up, w_exp_down, w_sh_gate, w_sh_up, w_sh_down,
           ln2_g, ln2_b):
    bsz, seq_len, d = x.shape
    depth = w_in.shape[0]
    t = bsz * seq_len
    nh, dh = MLSTM_HEADS, MLSTM_HEAD_DIM
    w_m = nh * dh
    w_s = SGU_GROUPS * SGU_GROUP_DIM
    off_g = 4 * w_m
    n_e, f_e = w_exp_gate.shape[1], w_exp_gate.shape[3]
    nc_total = t // CHUNK

    xf = x.reshape(t, d)
    xb = xf.astype(BF16)
    k_scale = jnp.concatenate([jnp.ones((1, w_m), F32), jnp.full((1, w_m), dh ** -0.5, F32)], axis=1)
    lane_pad = V7X_LANES - N_EXPERTS

    for l in range(depth):
        w_main = jnp.concatenate([w_in[l, :, :off_g], w_in[l, :, off_g + N_GATES:]], axis=1).astype(BF16)
        w_gate = jnp.pad(w_in[l, :, off_g:off_g + N_GATES], ((0, 0), (0, V7X_LANES - N_GATES))).astype(BF16)
        w_o = w_out[l].astype(BF16)
        w_r = jnp.pad(w_router[l], ((0, 0), (0, lane_pad))).astype(BF16)
        b_r = jnp.pad(b_router[l], (0, lane_pad)).reshape(1, V7X_LANES)
        wg_all = jnp.concatenate([w_exp_gate[l].transpose(1, 0, 2).reshape(d, n_e * f_e), w_sh_gate[l]],
                                 axis=1).astype(BF16)
        wu_all = jnp.concatenate([w_exp_up[l].transpose(1, 0, 2).reshape(d, n_e * f_e), w_sh_up[l]],
                                 axis=1).astype(BF16)
        wd_all = jnp.concatenate([w_exp_down[l].reshape(n_e * f_e, d), w_sh_down[l]], axis=0).astype(BF16)

        z = matmul(xb, w_main, tm=1024, tn=1024, out_dtype=BF16, name=f"in_proj_{l}")
        g_raw = matmul(xb, w_gate, tm=1024, tn=V7X_LANES, out_dtype=F32, name=f"gate_proj_{l}")
        g4 = g_raw[:, :N_GATES].T.reshape(4, nh, t)
        tables = gate_prep(g4, b_gates[l].reshape(4, nh, 1), tl=2048, name=f"gate_prep_{l}")
        rows = tables.transpose(1, 0, 2).reshape(nh, 8, nc_total, CHUNK).transpose(0, 2, 1, 3)
        cols = tables.reshape(8, nh, bsz, seq_len).transpose(2, 1, 3, 0)
        qk = conv_silu(z, conv_qk[l], k_scale, seq_len=seq_len, tr=512, tc=512, name=f"conv_silu_{l}")
        h_m = mlstm(qk, z, rows, cols, mlstm_norm_g[l], batch=bsz, seq_len=seq_len, name=f"mlstm_{l}")
        h_s = spatial_gating(z, sgu_norm_g[l], sgu_norm_b[l], w_spatial[l], b_spatial[l],
                             u_col=4 * w_m // SGU_GROUP_DIM, v_col=(4 * w_m + w_s) // SGU_GROUP_DIM,
                             tr=1024, name=f"sgu_{l}")
        mix = matmul_cat(h_m, h_s, w_o, tm=1024, tn=1024, out_dtype=F32, name=f"out_proj_{l}")
        x1, x1b = add_layer_norm(xf, mix, ln1_g[l], ln1_b[l], tm=256, name=f"ln1_{l}")

        comb = router(x1b, w_r, b_r, tm=512, name=f"router_{l}")
        y = ffn(x1b, comb, wg_all, wu_all, wd_all, tm=512, tf=256, name=f"ffn_{l}")
        xf, xb = add_layer_norm(x1, y, ln2_g[l], ln2_b[l], tm=256, name=f"ln2_{l}")

    return xf.reshape(bsz, seq_len, d)
```

```python
import functools
import math

import jax
import jax.numpy as jnp
from jax import lax
from jax.experimental import pallas as pl
from jax.experimental.pallas import tpu as pltpu

MLSTM_HEADS = 8
MLSTM_HEAD_DIM = 256
CHUNK = 128
SGU_GROUPS = 8
SGU_GROUP_DIM = 256
N_GATES = 4 * MLSTM_HEADS
N_EXPERTS = 64
TOP_K = 8
D_EXPERT = 128
ROUTE_SCALE = 2.5
DEPTH_FOR_NORM = 4
DEEPNORM_ALPHA = (2 * DEPTH_FOR_NORM) ** 0.25
LN_EPS = 1e-5

V7X_LANES = 128
V7X_VMEM_LIMIT_BYTES = 60000 * 1024

F32 = jnp.float32
BF16 = jnp.bfloat16


def _cparams(n_axes, vmem_bytes):
    limit = int(min(max(vmem_bytes * 5 // 4 + (4 << 20), 16 << 20), V7X_VMEM_LIMIT_BYTES))
    return pltpu.CompilerParams(dimension_semantics=("arbitrary",) * n_axes, vmem_limit_bytes=limit)


def _nbytes(shape, dtype):
    return math.prod(shape) * jnp.dtype(dtype).itemsize


def _mm_kernel(a_ref, b_ref, o_ref):
    o_ref[...] = jnp.dot(a_ref[...], b_ref[...], preferred_element_type=F32).astype(o_ref.dtype)


def matmul(a, b, *, tm, tn, out_dtype, name, n_cols=None):
    m, k = a.shape
    n = b.shape[1] if n_cols is None else n_cols
    tm, tn = min(tm, m), min(tn, n)
    assert m % tm == 0 and n % tn == 0
    vmem = 2 * (_nbytes((tm, k), a.dtype) + _nbytes((k, tn), b.dtype) + _nbytes((tm, tn), out_dtype))
    vmem += _nbytes((tm, tn), F32)
    return pl.pallas_call(
        _mm_kernel,
        grid=(n // tn, m // tm),
        in_specs=[pl.BlockSpec((tm, k), lambda j, i: (i, 0)),
                  pl.BlockSpec((k, tn), lambda j, i: (0, j))],
        out_specs=pl.BlockSpec((tm, tn), lambda j, i: (i, j)),
        out_shape=jax.ShapeDtypeStruct((m, n), out_dtype),
        compiler_params=_cparams(2, vmem),
        name=name,
    )(a, b)


def _mm2_kernel(a1_ref, a2_ref, b1_ref, b2_ref, o_ref):
    acc = jnp.dot(a1_ref[...], b1_ref[...], preferred_element_type=F32)
    acc = acc + jnp.dot(a2_ref[...], b2_ref[...], preferred_element_type=F32)
    o_ref[...] = acc.astype(o_ref.dtype)


def matmul_cat(a1, a2, b1, b2, *, tm, tn, out_dtype, name):
    m, k1 = a1.shape
    _, k2 = a2.shape
    n = b1.shape[1]
    assert b1.shape == (k1, n) and b2.shape == (k2, n)
    tm, tn = min(tm, m), min(tn, n)
    assert m % tm == 0 and n % tn == 0
    k = k1 + k2
    vmem = 2 * (_nbytes((tm, k), a1.dtype) + _nbytes((k, tn), b1.dtype) + _nbytes((tm, tn), out_dtype))
    vmem += _nbytes((tm, tn), F32)
    return pl.pallas_call(
        _mm2_kernel,
        grid=(n // tn, m // tm),
        in_specs=[pl.BlockSpec((tm, k1), lambda j, i: (i, 0)),
                  pl.BlockSpec((tm, k2), lambda j, i: (i, 0)),
                  pl.BlockSpec((k1, tn), lambda j, i: (0, j)),
                  pl.BlockSpec((k2, tn), lambda j, i: (0, j))],
        out_specs=pl.BlockSpec((tm, tn), lambda j, i: (i, j)),
        out_shape=jax.ShapeDtypeStruct((m, n), out_dtype),
        compiler_params=_cparams(2, vmem),
        name=name,
    )(a1, a2, b1, b2)


def _add_ln_kernel(x_ref, r_ref, g_ref, b_ref, y_ref, yb_ref):
    v = DEEPNORM_ALPHA * x_ref[...] + r_ref[...]
    mu = jnp.mean(v, axis=-1, keepdims=True)
    c = v - mu
    var = jnp.mean(c * c, axis=-1, keepdims=True)
    y = c * lax.rsqrt(var + LN_EPS) * g_ref[...] + b_ref[...]
    y_ref[...] = y
    yb_ref[...] = y.astype(BF16)


def add_layer_norm(x, r, g, b, *, tm, name):
    t, d = x.shape
    tm = min(tm, t)
    assert t % tm == 0
    vmem = 2 * (3 * _nbytes((tm, d), F32) + _nbytes((tm, d), BF16)) + 4 * _nbytes((tm, d), F32)
    row = lambda i: (i, 0)
    return pl.pallas_call(
        _add_ln_kernel,
        grid=(t // tm,),
        in_specs=[pl.BlockSpec((tm, d), row), pl.BlockSpec((tm, d), row),
                  pl.BlockSpec((1, d), lambda i: (0, 0)), pl.BlockSpec((1, d), lambda i: (0, 0))],
        out_specs=[pl.BlockSpec((tm, d), row), pl.BlockSpec((tm, d), row)],
        out_shape=[jax.ShapeDtypeStruct((t, d), F32), jax.ShapeDtypeStruct((t, d), BF16)],
        compiler_params=_cparams(1, vmem),
        name=name,
    )(x, r, g.reshape(1, d), b.reshape(1, d))


_HALO = 16


def _conv_silu_kernel(z_ref, zp_ref, zn_ref, w_ref, s_ref, o_ref, *, seq_len):
    tr = z_ref.shape[0]
    i = pl.program_id(0)
    x = z_ref[...].astype(F32)
    row = lax.broadcasted_iota(jnp.int32, x.shape, 0)
    t0 = i * tr
    prev_row = jnp.where(lax.rem(t0, seq_len) == 0, 0.0, zp_ref[_HALO - 1:_HALO, :].astype(F32))
    next_row = jnp.where(lax.rem(t0 + tr, seq_len) == 0, 0.0, zn_ref[0:1, :].astype(F32))
    x_prev = jnp.where(row == 0, prev_row, pltpu.roll(x, 1, axis=0))
    x_next = jnp.where(row == tr - 1, next_row, pltpu.roll(x, tr - 1, axis=0))
    y = w_ref[0:1, :] * x_prev + w_ref[1:2, :] * x + w_ref[2:3, :] * x_next
    o_ref[...] = (y * jax.nn.sigmoid(y) * s_ref[...]).astype(o_ref.dtype)


def conv_silu(z, conv_w, col_scale, *, seq_len, tr, tc, name):
    t = z.shape[0]
    c = conv_w.shape[1]
    tr = min(tr, seq_len)
    assert seq_len % tr == 0 and tr % _HALO == 0 and c % tc == 0
    hb = tr // _HALO
    n_hb = t // _HALO
    vmem = 2 * (2 * _nbytes((tr, tc), BF16) + 2 * _nbytes((_HALO, tc), BF16)) + 6 * _nbytes((tr, tc), F32)
    return pl.pallas_call(
        functools.partial(_conv_silu_kernel, seq_len=seq_len),
        grid=(t // tr, c // tc),
        in_specs=[pl.BlockSpec((tr, tc), lambda i, j: (i, j)),
                  pl.BlockSpec((_HALO, tc), lambda i, j: (jnp.maximum(i * hb - 1, 0), j)),
                  pl.BlockSpec((_HALO, tc), lambda i, j: (jnp.minimum((i + 1) * hb, n_hb - 1), j)),
                  pl.BlockSpec((3, tc), lambda i, j: (0, j)),
                  pl.BlockSpec((1, tc), lambda i, j: (0, j))],
        out_specs=pl.BlockSpec((tr, tc), lambda i, j: (i, j)),
        out_shape=jax.ShapeDtypeStruct((t, c), BF16),
        compiler_params=_cparams(2, vmem),
        name=name,
    )(z, z, z, conv_w, col_scale)


def _chunk_scan(x, lane_in_chunk, *, suffix):
    n = x.shape[-1]
    k = 1
    while k < CHUNK:
        if suffix:
            shifted = pltpu.roll(x, n - k, axis=1)
            x = x + jnp.where(lane_in_chunk < CHUNK - k, shifted, 0.0)
        else:
            shifted = pltpu.roll(x, k, axis=1)
            x = x + jnp.where(lane_in_chunk >= k, shifted, 0.0)
        k *= 2
    return x


def _gate_prep_kernel(g_ref, bias_ref, o_ref):
    i_f = g_ref[0] + bias_ref[0]
    lf_f = jax.nn.log_sigmoid(g_ref[1] + bias_ref[1])
    i_b = g_ref[2] + bias_ref[2]
    lf_b = jax.nn.log_sigmoid(g_ref[3] + bias_ref[3])
    lane = lax.rem(lax.broadcasted_iota(jnp.int32, i_f.shape, 1), CHUNK)
    b_f = _chunk_scan(lf_f, lane, suffix=False)
    g_f = b_f + _chunk_scan(lf_f, lane, suffix=True) - lf_f
    b_b = _chunk_scan(lf_b, lane, suffix=True)
    g_b = b_b + _chunk_scan(lf_b, lane, suffix=False) - lf_b
    r_f = i_f - b_f
    r_b = i_b - b_b
    o_ref[0] = b_f
    o_ref[1] = r_f
    o_ref[2] = g_f + r_f
    o_ref[3] = g_f
    o_ref[4] = b_b
    o_ref[5] = r_b
    o_ref[6] = g_b + r_b
    o_ref[7] = g_b


def gate_prep(g4, bias, *, tl, name):
    _, h, t = g4.shape
    tl = min(tl, t)
    assert t % tl == 0 and tl % CHUNK == 0
    vmem = 2 * (_nbytes((4, h, tl), F32) + _nbytes((8, h, tl), F32)) + 16 * _nbytes((h, tl), F32)
    return pl.pallas_call(
        _gate_prep_kernel,
        grid=(t // tl,),
        in_specs=[pl.BlockSpec((4, h, tl), lambda i: (0, 0, i)),
                  pl.BlockSpec((4, h, 1), lambda i: (0, 0, 0))],
        out_specs=pl.BlockSpec((8, h, tl), lambda i: (0, 0, i)),
        out_shape=jax.ShapeDtypeStruct((8, h, t), F32),
        compiler_params=_cparams(1, vmem),
        name=name,
    )(g4, bias)


_NT = (((1,), (1,)), ((), ()))
_TN = (((0,), (0,)), ((), ()))


def _mlstm_chunk(q, k, v, b_col, a_col, g11, r_row, ct_ref, n_ref, m11, mask):
    dmat = jnp.where(mask, b_col + r_row, -jnp.inf)
    inter = b_col + m11
    m_t = jnp.maximum(inter, jnp.max(dmat, axis=1, keepdims=True))
    w_intra = jnp.exp(dmat - m_t)
    w_inter = jnp.exp(inter - m_t)
    s_qk = lax.dot_general(q, k, _NT, preferred_element_type=F32) * w_intra
    ct = ct_ref[...]
    num = jnp.dot(s_qk.astype(BF16), v, preferred_element_type=F32)
    num = num + w_inter * jnp.dot(q, ct.astype(BF16), preferred_element_type=F32)
    qn = jnp.sum(q.astype(F32) * n_ref[...], axis=1, keepdims=True)
    den = jnp.sum(s_qk, axis=1, keepdims=True) + w_inter * qn
    h = num / jnp.maximum(jnp.abs(den), jnp.exp(-m_t))
    m_new = jnp.maximum(g11 + m11, jnp.max(a_col, axis=0, keepdims=True))
    wa = jnp.exp(a_col - m_new)
    decay = jnp.exp(g11 + m11 - m_new)
    kw = k.astype(F32) * wa
    ct_ref[...] = decay * ct + lax.dot_general(kw.astype(BF16), v, _TN, preferred_element_type=F32)
    n_ref[...] = decay * n_ref[...] + jnp.sum(kw, axis=0, keepdims=True)
    return h, m_new


def _mlstm_kernel(q_ref, k_ref, v_ref, o_ref, rows_ref, cols_ref, hg_ref, out_ref,
                  acc_ref, ctf_ref, ctb_ref, nf_ref, nb_ref):
    nc = rows_ref.shape[0]
    L = CHUNK
    ctf_ref[...] = jnp.zeros_like(ctf_ref)
    ctb_ref[...] = jnp.zeros_like(ctb_ref)
    nf_ref[...] = jnp.zeros_like(nf_ref)
    nb_ref[...] = jnp.zeros_like(nb_ref)
    t_idx = lax.broadcasted_iota(jnp.int32, (L, L), 0)
    s_idx = lax.broadcasted_iota(jnp.int32, (L, L), 1)
    causal = s_idx <= t_idx
    anti = s_idx >= t_idx

    def directions(c, m_f, m_b):
        cb = nc - 1 - c
        sf = pl.ds(pl.multiple_of(c * L, L), L)
        sb = pl.ds(pl.multiple_of(cb * L, L), L)
        cols_f = cols_ref[sf, :]
        cols_b = cols_ref[sb, :]
        rows_f = rows_ref[c]
        rows_b = rows_ref[cb]
        h_f, m_f = _mlstm_chunk(q_ref[sf, :], k_ref[sf, :], v_ref[sf, :],
                                cols_f[:, 0:1], cols_f[:, 2:3], cols_f[L - 1:L, 3:4], rows_f[1:2, :],
                                ctf_ref, nf_ref, m_f, causal)
        h_b, m_b = _mlstm_chunk(q_ref[sb, :], k_ref[sb, :], v_ref[sb, :],
                                cols_b[:, 4:5], cols_b[:, 6:7], cols_b[0:1, 7:8], rows_b[5:6, :],
                                ctb_ref, nb_ref, m_b, anti)
        return sf, sb, h_f, h_b, m_f, m_b

    def first_half(c, carry):
        sf, sb, h_f, h_b, m_f, m_b = directions(c, *carry)
        acc_ref[sf, :] = h_f
        acc_ref[sb, :] = h_b
        return m_f, m_b

    def finish(sl, h_new):
        hh = (acc_ref[sl, :] + h_new) * jax.nn.sigmoid(o_ref[sl, :].astype(F32))
        mu = jnp.mean(hh, axis=1, keepdims=True)
        cen = hh - mu
        var = jnp.mean(cen * cen, axis=1, keepdims=True)
        out_ref[sl, :] = (cen * lax.rsqrt(var + LN_EPS) * hg_ref[...]).astype(out_ref.dtype)

    def second_half(c, carry):
        sf, sb, h_f, h_b, m_f, m_b = directions(c, *carry)
        finish(sf, h_f)
        finish(sb, h_b)
        return m_f, m_b

    zero = jnp.zeros((1, 1), F32)
    carry = lax.fori_loop(0, nc // 2, first_half, (zero, zero))
    lax.fori_loop(nc // 2, nc, second_half, carry)


def mlstm(qk, z, rows, cols, head_g, *, batch, seq_len, name):
    t = qk.shape[0]
    nh, dh = MLSTM_HEADS, MLSTM_HEAD_DIM
    nc = seq_len // CHUNK
    assert nc % 2 == 0
    single = dict(pipeline_mode=pl.Buffered(1))
    vmem = 4 * _nbytes((seq_len, dh), BF16) + _nbytes((seq_len, V7X_LANES), F32)
    vmem += 2 * _nbytes((seq_len, dh), BF16) + _nbytes((seq_len, dh), F32) + 2 * _nbytes((nc, 8, CHUNK), F32)
    vmem += 4 * _nbytes((dh, dh), F32)
    return pl.pallas_call(
        _mlstm_kernel,
        grid=(batch, nh),
        in_specs=[pl.BlockSpec((seq_len, dh), lambda b, h: (b, h), **single),
                  pl.BlockSpec((seq_len, dh), lambda b, h: (b, nh + h), **single),
                  pl.BlockSpec((seq_len, dh), lambda b, h: (b, 2 * nh + h), **single),
                  pl.BlockSpec((seq_len, dh), lambda b, h: (b, 3 * nh + h), **single),
                  pl.BlockSpec((None, nc, 8, CHUNK), lambda b, h: (h, b, 0, 0)),
                  pl.BlockSpec((None, None, seq_len, 8), lambda b, h: (b, h, 0, 0), **single),
                  pl.BlockSpec((None, 1, dh), lambda b, h: (h, 0, 0))],
        out_specs=pl.BlockSpec((seq_len, dh), lambda b, h: (b, h)),
        out_shape=jax.ShapeDtypeStruct((t, nh * dh), BF16),
        scratch_shapes=[pltpu.VMEM((seq_len, dh), F32),
                        pltpu.VMEM((dh, dh), F32), pltpu.VMEM((dh, dh), F32),
                        pltpu.VMEM((1, dh), F32), pltpu.VMEM((1, dh), F32)],
        compiler_params=_cparams(2, vmem),
        name=name,
    )(qk, qk, z, z, rows, cols, head_g.reshape(nh, 1, dh))


def _gelu_tanh(x):
    return 0.5 * x * (1.0 + jnp.tanh(math.sqrt(2.0 / math.pi) * (x + 0.044715 * (x * x * x))))


def _sgu_kernel(u_ref, v_ref, ng_ref, nb_ref, ws_ref, bs_ref, o_ref):
    n_chunks = u_ref.shape[0] // CHUNK
    ws = ws_ref[...]
    for c in range(n_chunks):
        sl = slice(c * CHUNK, (c + 1) * CHUNK)
        v = _gelu_tanh(v_ref[sl, :].astype(F32))
        mu = jnp.mean(v, axis=1, keepdims=True)
        cen = v - mu
        var = jnp.mean(cen * cen, axis=1, keepdims=True)
        vn = cen * lax.rsqrt(var + LN_EPS) * ng_ref[...] + nb_ref[...]
        sp = jnp.dot(ws, vn.astype(BF16), preferred_element_type=F32) + bs_ref[...]
        o_ref[sl, :] = (_gelu_tanh(u_ref[sl, :].astype(F32)) * sp).astype(o_ref.dtype)


def spatial_gating(z, norm_g, norm_b, w_s, b_s, *, u_col, v_col, tr, name):
    t = z.shape[0]
    g, dg = SGU_GROUPS, SGU_GROUP_DIM
    tr = min(tr, t)
    assert t % tr == 0 and tr % CHUNK == 0
    vmem = 2 * 3 * _nbytes((tr, dg), BF16) + 8 * _nbytes((CHUNK, dg), F32) * (tr // CHUNK)
    return pl.pallas_call(
        _sgu_kernel,
        grid=(g, t // tr),
        in_specs=[pl.BlockSpec((tr, dg), lambda j, i: (i, u_col + j)),
                  pl.BlockSpec((tr, dg), lambda j, i: (i, v_col + j)),
                  pl.BlockSpec((None, 1, dg), lambda j, i: (j, 0, 0)),
                  pl.BlockSpec((None, 1, dg), lambda j, i: (j, 0, 0)),
                  pl.BlockSpec((None, CHUNK, CHUNK), lambda j, i: (j, 0, 0)),
                  pl.BlockSpec((None, CHUNK, 1), lambda j, i: (j, 0, 0))],
        out_specs=pl.BlockSpec((tr, dg), lambda j, i: (i, j)),
        out_shape=jax.ShapeDtypeStruct((t, g * dg), BF16),
        compiler_params=_cparams(2, vmem),
        name=name,
    )(z, z, norm_g.reshape(g, 1, dg), norm_b.reshape(g, 1, dg), w_s.astype(BF16), b_s.reshape(g, CHUNK, 1))


def _router_kernel(x_ref, w_ref, b_ref, comb_ref):
    logits = jnp.dot(x_ref[...], w_ref[...], preferred_element_type=F32)
    lane = lax.broadcasted_iota(jnp.int32, logits.shape, 1)
    valid = lane < N_EXPERTS
    scores = jax.nn.sigmoid(logits)
    sel = jnp.where(valid, scores + b_ref[...], -jnp.inf)
    picked = jnp.zeros(logits.shape, jnp.bool_)
    for _ in range(TOP_K):
        best = jnp.max(sel, axis=1, keepdims=True)
        first = jnp.min(jnp.where(sel == best, lane, V7X_LANES), axis=1, keepdims=True)
        hit = lane == first
        picked = jnp.logical_or(picked, hit)
        sel = jnp.where(hit, -jnp.inf, sel)
    w = jnp.where(picked, scores, 0.0)
    comb_ref[...] = w / jnp.sum(w, axis=1, keepdims=True) * ROUTE_SCALE


def router(xb, w_r, b_r, *, tm, name):
    t, d = xb.shape
    tm = min(tm, t)
    assert t % tm == 0
    vmem = 2 * (_nbytes((tm, d), BF16) + _nbytes((d, V7X_LANES), BF16) + _nbytes((tm, V7X_LANES), F32))
    vmem += 8 * _nbytes((tm, V7X_LANES), F32)
    return pl.pallas_call(
        _router_kernel,
        grid=(t // tm,),
        in_specs=[pl.BlockSpec((tm, d), lambda i: (i, 0)),
                  pl.BlockSpec((d, V7X_LANES), lambda i: (0, 0)),
                  pl.BlockSpec((1, V7X_LANES), lambda i: (0, 0))],
        out_specs=pl.BlockSpec((tm, V7X_LANES), lambda i: (i, 0)),
        out_shape=jax.ShapeDtypeStruct((t, V7X_LANES), F32),
        compiler_params=_cparams(1, vmem),
        name=name,
    )(xb, w_r, b_r)


def _swiglu(x, wg, wu):
    gate = jnp.dot(x, wg, preferred_element_type=F32)
    up = jnp.dot(x, wu, preferred_element_type=F32)
    return gate * jax.nn.sigmoid(gate) * up


def _routed_up_kernel(x_ref, comb_ref, wg_ref, wu_ref, h_ref, wgc_ref, wuc_ref):
    per = wg_ref.shape[0]
    fe = wg_ref.shape[2]
    j = pl.program_id(0)

    @pl.when(pl.program_id(1) == 0)
    def _():
        for e in range(per):
            wgc_ref[:, e * fe:(e + 1) * fe] = wg_ref[e]
            wuc_ref[:, e * fe:(e + 1) * fe] = wu_ref[e]

    h = _swiglu(x_ref[...], wgc_ref[...], wuc_ref[...])
    e_row = lax.broadcasted_iota(jnp.int32, (V7X_LANES, per * fe), 0)
    e_col = lax.broadcasted_iota(jnp.int32, (V7X_LANES, per * fe), 1) // fe + j * per
    expand = jnp.where(e_row == e_col, 1.0, 0.0).astype(BF16)
    comb = comb_ref[...]
    c_hi = comb.astype(BF16)
    c_lo = (comb - c_hi.astype(F32)).astype(BF16)
    scale = jnp.dot(c_hi, expand, preferred_element_type=F32) + jnp.dot(c_lo, expand, preferred_element_type=F32)
    h_ref[...] = (h * scale).astype(h_ref.dtype)


def routed_up(xb, comb, wg, wu, *, tm, per, name):
    t, d = xb.shape
    n_e, _, fe = wg.shape
    tm = min(tm, t)
    assert t % tm == 0 and n_e % per == 0
    tf = per * fe
    vmem = 2 * (_nbytes((tm, d), BF16) + _nbytes((tm, V7X_LANES), F32) + 2 * _nbytes((d, tf), BF16))
    vmem += 2 * _nbytes((tm, tf), BF16) + 2 * _nbytes((d, tf), BF16) + 5 * _nbytes((tm, tf), F32)
    return pl.pallas_call(
        _routed_up_kernel,
        grid=(n_e // per, t // tm),
        in_specs=[pl.BlockSpec((tm, d), lambda j, i: (i, 0)),
                  pl.BlockSpec((tm, V7X_LANES), lambda j, i: (i, 0)),
                  pl.BlockSpec((per, d, fe), lambda j, i: (j, 0, 0)),
                  pl.BlockSpec((per, d, fe), lambda j, i: (j, 0, 0))],
        out_specs=pl.BlockSpec((tm, tf), lambda j, i: (i, j)),
        out_shape=jax.ShapeDtypeStruct((t, n_e * fe), BF16),
        scratch_shapes=[pltpu.VMEM((d, tf), BF16), pltpu.VMEM((d, tf), BF16)],
        compiler_params=_cparams(2, vmem),
        name=name,
    )(xb, comb, wg, wu)


def _shared_up_kernel(x_ref, wg_ref, wu_ref, h_ref):
    h_ref[...] = _swiglu(x_ref[...], wg_ref[...], wu_ref[...]).astype(h_ref.dtype)


def shared_up(xb, wg, wu, *, tm, tf, name):
    t, d = xb.shape
    f = wg.shape[1]
    tm, tf = min(tm, t), min(tf, f)
    assert t % tm == 0 and f % tf == 0
    vmem = 2 * (_nbytes((tm, d), BF16) + 2 * _nbytes((d, tf), BF16) + _nbytes((tm, tf), BF16))
    vmem += 4 * _nbytes((tm, tf), F32)
    return pl.pallas_call(
        _shared_up_kernel,
        grid=(f // tf, t // tm),
        in_specs=[pl.BlockSpec((tm, d), lambda j, i: (i, 0)),
                  pl.BlockSpec((d, tf), lambda j, i: (0, j)),
                  pl.BlockSpec((d, tf), lambda j, i: (0, j))],
        out_specs=pl.BlockSpec((tm, tf), lambda j, i: (i, j)),
        out_shape=jax.ShapeDtypeStruct((t, f), BF16),
        compiler_params=_cparams(2, vmem),
        name=name,
    )(xb, wg, wu)


def kernel(x, w_in, b_gates, conv_qk, mlstm_norm_g, sgu_norm_g, sgu_norm_b, w_spatial, b_spatial, w_out,
           ln1_g, ln1_b, w_router, b_router, w_exp_gate, w_exp_up, w_exp_down, w_sh_gate, w_sh_up, w_sh_down,
           ln2_g, ln2_b):
    bsz, seq_len, d = x.shape
    depth = w_in.shape[0]
    t = bsz * seq_len
    nh, dh = MLSTM_HEADS, MLSTM_HEAD_DIM
    w_m = nh * dh
    w_s = SGU_GROUPS * SGU_GROUP_DIM
    off_g = 4 * w_m
    n_e, f_e = w_exp_gate.shape[1], w_exp_gate.shape[3]
    nc_total = t // CHUNK

    xf = x.reshape(t, d)
    xb = xf.astype(BF16)
    k_scale = jnp.concatenate([jnp.ones((1, w_m), F32), jnp.full((1, w_m), dh ** -0.5, F32)], axis=1)
    lane_pad = V7X_LANES - N_EXPERTS

    for l in range(depth):
        w_in_b = w_in[l].astype(BF16)
        w_usv = w_in[l, :, off_g + N_GATES:].astype(BF16)
        w_gate = jnp.pad(w_in[l, :, off_g:off_g + N_GATES], ((0, 0), (0, V7X_LANES - N_GATES))).astype(BF16)
        w_r = jnp.pad(w_router[l], ((0, 0), (0, lane_pad))).astype(BF16)
        b_r = jnp.pad(b_router[l], (0, lane_pad)).reshape(1, V7X_LANES)

        z_m = matmul(xb, w_in_b, tm=1024, tn=1024, out_dtype=BF16, n_cols=off_g, name=f"in_proj_m_{l}")
        z_s = matmul(xb, w_usv, tm=1024, tn=1024, out_dtype=BF16, name=f"in_proj_s_{l}")
        g_raw = matmul(xb, w_gate, tm=1024, tn=V7X_LANES, out_dtype=F32, name=f"gate_proj_{l}")
        g4 = g_raw[:, :N_GATES].T.reshape(4, nh, t)
        tables = gate_prep(g4, b_gates[l].reshape(4, nh, 1), tl=2048, name=f"gate_prep_{l}")
        rows = tables.transpose(1, 0, 2).reshape(nh, 8, nc_total, CHUNK).transpose(0, 2, 1, 3)
        cols = tables.reshape(8, nh, bsz, seq_len).transpose(2, 1, 3, 0)
        qk = conv_silu(z_m, conv_qk[l], k_scale, seq_len=seq_len, tr=512, tc=512, name=f"conv_silu_{l}")
        h_m = mlstm(qk, z_m, rows, cols, mlstm_norm_g[l], batch=bsz, seq_len=seq_len, name=f"mlstm_{l}")
        h_s = spatial_gating(z_s, sgu_norm_g[l], sgu_norm_b[l], w_spatial[l], b_spatial[l],
                             u_col=0, v_col=w_s // SGU_GROUP_DIM, tr=1024, name=f"sgu_{l}")
        w_o = w_out[l].astype(BF16)
        mix = matmul_cat(h_m, h_s, w_o[:w_m], w_o[w_m:], tm=1024, tn=1024, out_dtype=F32, name=f"out_proj_{l}")
        x1, x1b = add_layer_norm(xf, mix, ln1_g[l], ln1_b[l], tm=256, name=f"ln1_{l}")

        comb = router(x1b, w_r, b_r, tm=512, name=f"router_{l}")
        hid_r = routed_up(x1b, comb, w_exp_gate[l].astype(BF16), w_exp_up[l].astype(BF16),
                          tm=1024, per=4, name=f"routed_up_{l}")
        hid_s = shared_up(x1b, w_sh_gate[l].astype(BF16), w_sh_up[l].astype(BF16), tm=1024, tf=512,
                          name=f"shared_up_{l}")
        y = matmul_cat(hid_r, hid_s, w_exp_down[l].reshape(n_e * f_e, d).astype(BF16), w_sh_down[l].astype(BF16),
                       tm=512, tn=512, out_dtype=F32, name=f"ffn_down_{l}")
        xf, xb = add_layer_norm(x1, y, ln2_g[l], ln2_b[l], tm=256, name=f"ln2_{l}")

    return xf.reshape(bsz, seq_len, d)
```

```python
import functools
import math

import jax
import jax.numpy as jnp
from jax import lax
from jax.experimental import pallas as pl
from jax.experimental.pallas import tpu as pltpu

MLSTM_HEADS = 8
MLSTM_HEAD_DIM = 256
CHUNK = 128
SGU_GROUPS = 8
SGU_GROUP_DIM = 256
N_GATES = 4 * MLSTM_HEADS
N_EXPERTS = 64
TOP_K = 8
D_EXPERT = 128
ROUTE_SCALE = 2.5
DEPTH_FOR_NORM = 4
DEEPNORM_ALPHA = (2 * DEPTH_FOR_NORM) ** 0.25
LN_EPS = 1e-5

V7X_LANES = 128
V7X_VMEM_LIMIT_BYTES = 60000 * 1024

F32 = jnp.float32
BF16 = jnp.bfloat16


def _cparams(n_axes, vmem_bytes):
    limit = int(min(max(vmem_bytes * 5 // 4 + (4 << 20), 16 << 20), V7X_VMEM_LIMIT_BYTES))
    return pltpu.CompilerParams(dimension_semantics=("arbitrary",) * n_axes, vmem_limit_bytes=limit)


def _nbytes(shape, dtype):
    return math.prod(shape) * jnp.dtype(dtype).itemsize


def _weight_spec(w, rows, tn, layer, row_block=0):
    mode = dict(pipeline_mode=pl.Buffered(1)) if w.dtype != BF16 else {}
    if w.ndim == 2:
        return pl.BlockSpec((rows, tn), lambda j, i: (row_block, j), **mode)
    return pl.BlockSpec((None, rows, tn), lambda j, i: (layer, row_block, j), **mode)


def _mm_kernel(a_ref, b_ref, o_ref, *scratch):
    if scratch:
        (wb_ref,) = scratch

        @pl.when(pl.program_id(1) == 0)
        def _():
            wb_ref[...] = b_ref[...].astype(BF16)

        b = wb_ref[...]
    else:
        b = b_ref[...]
    o_ref[...] = jnp.dot(a_ref[...], b, preferred_element_type=F32).astype(o_ref.dtype)


def matmul(a, b, *, tm, tn, out_dtype, name, n_cols=None, layer=None):
    m, k = a.shape
    n = b.shape[-1] if n_cols is None else n_cols
    tm, tn = min(tm, m), min(tn, n)
    assert m % tm == 0 and n % tn == 0
    cast = b.dtype != BF16
    vmem = 2 * (_nbytes((tm, k), a.dtype) + _nbytes((tm, tn), out_dtype)) + _nbytes((tm, tn), F32)
    vmem += _nbytes((k, tn), F32) + _nbytes((k, tn), BF16) if cast else 2 * _nbytes((k, tn), BF16)
    return pl.pallas_call(
        _mm_kernel,
        grid=(n // tn, m // tm),
        in_specs=[pl.BlockSpec((tm, k), lambda j, i: (i, 0)), _weight_spec(b, k, tn, layer)],
        out_specs=pl.BlockSpec((tm, tn), lambda j, i: (i, j)),
        out_shape=jax.ShapeDtypeStruct((m, n), out_dtype),
        scratch_shapes=[pltpu.VMEM((k, tn), BF16)] if cast else [],
        compiler_params=_cparams(2, vmem),
        name=name,
    )(a, b)


def _mm2_kernel(a1_ref, a2_ref, b1_ref, b2_ref, o_ref, *scratch):
    if scratch:
        w1_ref, w2_ref = scratch

        @pl.when(pl.program_id(1) == 0)
        def _():
            w1_ref[...] = b1_ref[...].astype(BF16)
            w2_ref[...] = b2_ref[...].astype(BF16)

        b1, b2 = w1_ref[...], w2_ref[...]
    else:
        b1, b2 = b1_ref[...], b2_ref[...]
    acc = jnp.dot(a1_ref[...], b1, preferred_element_type=F32)
    acc = acc + jnp.dot(a2_ref[...], b2, preferred_element_type=F32)
    o_ref[...] = acc.astype(o_ref.dtype)


def matmul_cat(a1, a2, b1, b2, *, tm, tn, out_dtype, name, layer=None, row_blocks=(0, 0)):
    m, k1 = a1.shape
    _, k2 = a2.shape
    n = b1.shape[-1]
    tm, tn = min(tm, m), min(tn, n)
    assert m % tm == 0 and n % tn == 0
    k = k1 + k2
    assert row_blocks[0] == row_blocks[1] or k1 == k2
    cast = b1.dtype != BF16
    vmem = 2 * (_nbytes((tm, k), a1.dtype) + _nbytes((tm, tn), out_dtype)) + _nbytes((tm, tn), F32)
    vmem += _nbytes((k, tn), F32) + _nbytes((k, tn), BF16) if cast else 2 * _nbytes((k, tn), BF16)
    return pl.pallas_call(
        _mm2_kernel,
        grid=(n // tn, m // tm),
        in_specs=[pl.BlockSpec((tm, k1), lambda j, i: (i, 0)),
                  pl.BlockSpec((tm, k2), lambda j, i: (i, 0)),
                  _weight_spec(b1, k1, tn, layer, row_blocks[0]),
                  _weight_spec(b2, k2, tn, layer, row_blocks[1])],
        out_specs=pl.BlockSpec((tm, tn), lambda j, i: (i, j)),
        out_shape=jax.ShapeDtypeStruct((m, n), out_dtype),
        scratch_shapes=[pltpu.VMEM((k1, tn), BF16), pltpu.VMEM((k2, tn), BF16)] if cast else [],
        compiler_params=_cparams(2, vmem),
        name=name,
    )(a1, a2, b1, b2)


def _add_ln_kernel(x_ref, r_ref, g_ref, b_ref, y_ref, yb_ref):
    v = DEEPNORM_ALPHA * x_ref[...] + r_ref[...].astype(F32)
    mu = jnp.mean(v, axis=-1, keepdims=True)
    c = v - mu
    var = jnp.mean(c * c, axis=-1, keepdims=True)
    y = c * lax.rsqrt(var + LN_EPS) * g_ref[...] + b_ref[...]
    y_ref[...] = y
    yb_ref[...] = y.astype(BF16)


def add_layer_norm(x, r, g, b, *, tm, name):
    t, d = x.shape
    tm = min(tm, t)
    assert t % tm == 0
    vmem = 2 * (3 * _nbytes((tm, d), F32) + _nbytes((tm, d), BF16)) + 4 * _nbytes((tm, d), F32)
    row = lambda i: (i, 0)
    return pl.pallas_call(
        _add_ln_kernel,
        grid=(t // tm,),
        in_specs=[pl.BlockSpec((tm, d), row), pl.BlockSpec((tm, d), row),
                  pl.BlockSpec((1, d), lambda i: (0, 0)), pl.BlockSpec((1, d), lambda i: (0, 0))],
        out_specs=[pl.BlockSpec((tm, d), row), pl.BlockSpec((tm, d), row)],
        out_shape=[jax.ShapeDtypeStruct((t, d), F32), jax.ShapeDtypeStruct((t, d), BF16)],
        compiler_params=_cparams(1, vmem),
        name=name,
    )(x, r, g.reshape(1, d), b.reshape(1, d))


_HALO = 16


def _conv3_silu(z_ref, zp_ref, zn_ref, w_ref, seq_len):
    tr = z_ref.shape[0]
    x = z_ref[...].astype(F32)
    row = lax.broadcasted_iota(jnp.int32, x.shape, 0)
    t0 = pl.program_id(0) * tr
    prev_row = jnp.where(lax.rem(t0, seq_len) == 0, 0.0, zp_ref[_HALO - 1:_HALO, :].astype(F32))
    next_row = jnp.where(lax.rem(t0 + tr, seq_len) == 0, 0.0, zn_ref[0:1, :].astype(F32))
    x_prev = jnp.where(row == 0, prev_row, pltpu.roll(x, 1, axis=0))
    x_next = jnp.where(row == tr - 1, next_row, pltpu.roll(x, tr - 1, axis=0))
    y = w_ref[0:1, :] * x_prev + w_ref[1:2, :] * x + w_ref[2:3, :] * x_next
    return y * jax.nn.sigmoid(y)


def _conv_silu_kernel(q_ref, qp_ref, qn_ref, k_ref, kp_ref, kn_ref, wq_ref, wk_ref, oq_ref, okt_ref,
                      *, seq_len, k_scale):
    oq_ref[...] = _conv3_silu(q_ref, qp_ref, qn_ref, wq_ref, seq_len).astype(oq_ref.dtype)
    k = _conv3_silu(k_ref, kp_ref, kn_ref, wk_ref, seq_len) * k_scale
    for c in range(okt_ref.shape[0]):
        okt_ref[c] = k[c * CHUNK:(c + 1) * CHUNK, :].T.astype(okt_ref.dtype)


def conv_silu(z, conv_w, *, width, seq_len, k_scale, tr, tc, name):
    t = z.shape[0]
    tr = min(tr, seq_len)
    assert seq_len % tr == 0 and tr % CHUNK == 0 and width % tc == 0
    hb = tr // _HALO
    n_hb = t // _HALO
    nkb = width // tc
    vmem = 2 * (3 * _nbytes((tr, tc), BF16) + 4 * _nbytes((_HALO, tc), BF16) + _nbytes((tr, tc), BF16))
    vmem += 12 * _nbytes((tr, tc), F32)
    prev = lambda off: (lambda i, j: (jnp.maximum(i * hb - 1, 0), j + off))
    nxt = lambda off: (lambda i, j: (jnp.minimum((i + 1) * hb, n_hb - 1), j + off))
    return pl.pallas_call(
        functools.partial(_conv_silu_kernel, seq_len=seq_len, k_scale=k_scale),
        grid=(t // tr, nkb),
        in_specs=[pl.BlockSpec((tr, tc), lambda i, j: (i, j)),
                  pl.BlockSpec((_HALO, tc), prev(0)),
                  pl.BlockSpec((_HALO, tc), nxt(0)),
                  pl.BlockSpec((tr, tc), lambda i, j: (i, j + nkb)),
                  pl.BlockSpec((_HALO, tc), prev(nkb)),
                  pl.BlockSpec((_HALO, tc), nxt(nkb)),
                  pl.BlockSpec((3, tc), lambda i, j: (0, j)),
                  pl.BlockSpec((3, tc), lambda i, j: (0, j + nkb))],
        out_specs=[pl.BlockSpec((tr, tc), lambda i, j: (i, j)),
                   pl.BlockSpec((tr // CHUNK, tc, CHUNK), lambda i, j: (i, j, 0))],
        out_shape=[jax.ShapeDtypeStruct((t, width), BF16),
                   jax.ShapeDtypeStruct((t // CHUNK, width, CHUNK), BF16)],
        compiler_params=_cparams(2, vmem),
        name=name,
    )(z, z, z, z, z, z, conv_w, conv_w)


def _chunk_scan(x, lane_in_chunk, *, suffix):
    n = x.shape[-1]
    k = 1
    while k < CHUNK:
        if suffix:
            shifted = pltpu.roll(x, n - k, axis=1)
            x = x + jnp.where(lane_in_chunk < CHUNK - k, shifted, 0.0)
        else:
            shifted = pltpu.roll(x, k, axis=1)
            x = x + jnp.where(lane_in_chunk >= k, shifted, 0.0)
        k *= 2
    return x


def _gate_prep_kernel(g_ref, bias_ref, o_ref):
    i_f = g_ref[0] + bias_ref[0]
    lf_f = jax.nn.log_sigmoid(g_ref[1] + bias_ref[1])
    i_b = g_ref[2] + bias_ref[2]
    lf_b = jax.nn.log_sigmoid(g_ref[3] + bias_ref[3])
    lane = lax.rem(lax.broadcasted_iota(jnp.int32, i_f.shape, 1), CHUNK)
    b_f = _chunk_scan(lf_f, lane, suffix=False)
    g_f = b_f + _chunk_scan(lf_f, lane, suffix=True) - lf_f
    b_b = _chunk_scan(lf_b, lane, suffix=True)
    g_b = b_b + _chunk_scan(lf_b, lane, suffix=False) - lf_b
    r_f = i_f - b_f
    r_b = i_b - b_b
    o_ref[0] = b_f
    o_ref[1] = r_f
    o_ref[2] = g_f + r_f
    o_ref[3] = g_f
    o_ref[4] = b_b
    o_ref[5] = r_b
    o_ref[6] = g_b + r_b
    o_ref[7] = g_b


def gate_prep(g4, bias, *, tl, name):
    _, h, t = g4.shape
    tl = min(tl, t)
    assert t % tl == 0 and tl % CHUNK == 0
    vmem = 2 * (_nbytes((4, h, tl), F32) + _nbytes((8, h, tl), F32)) + 16 * _nbytes((h, tl), F32)
    return pl.pallas_call(
        _gate_prep_kernel,
        grid=(t // tl,),
        in_specs=[pl.BlockSpec((4, h, tl), lambda i: (0, 0, i)),
                  pl.BlockSpec((4, h, 1), lambda i: (0, 0, 0))],
        out_specs=pl.BlockSpec((8, h, tl), lambda i: (0, 0, i)),
        out_shape=jax.ShapeDtypeStruct((8, h, t), F32),
        compiler_params=_cparams(1, vmem),
        name=name,
    )(g4, bias)


def _lanes(x, n):
    return x if n == V7X_LANES else jnp.concatenate([x] * (n // V7X_LANES), axis=1)


def _mlstm_chunk(q, kt, v_ext, b_col, r_row, a_row, st_ref, m11, mask):
    dv = v_ext.shape[1] - V7X_LANES
    b_wide = jnp.broadcast_to(b_col, (b_col.shape[0], V7X_LANES))
    dmat = jnp.where(mask, b_wide + r_row, -jnp.inf)
    inter = b_wide + m11
    m_t = jnp.maximum(inter, jnp.max(dmat, axis=1, keepdims=True))
    w_intra = jnp.exp(dmat - m_t)
    w_inter = jnp.exp(inter - m_t)
    s_qk = jnp.dot(q, kt, preferred_element_type=F32) * w_intra
    intra = jnp.dot(s_qk.astype(BF16), v_ext, preferred_element_type=F32)
    state = st_ref[...]
    carried = jnp.dot(q, state.astype(BF16), preferred_element_type=F32)
    num = intra[:, :dv] + _lanes(w_inter, dv) * carried[:, :dv]
    den = intra[:, dv:] + w_inter * carried[:, dv:]
    h = num / _lanes(jnp.maximum(jnp.abs(den), jnp.exp(-m_t)), dv)
    g11 = a_row[:, 0:1] - r_row[:, 0:1]
    m_new = jnp.maximum(g11 + m11, jnp.max(a_row, axis=1, keepdims=True))
    wa = jnp.exp(a_row - m_new)
    decay = jnp.exp(g11 + m11 - m_new)
    ktw = (kt.astype(F32) * wa).astype(BF16)
    st_ref[...] = decay * state + jnp.dot(ktw, v_ext, preferred_element_type=F32)
    return h, m_new


def _mlstm_kernel(q_ref, kt_ref, v_ref, o_ref, rows_ref, cols_ref, hg_ref, out_ref, acc_ref, stf_ref, stb_ref):
    nc = rows_ref.shape[0]
    L = CHUNK
    stf_ref[...] = jnp.zeros_like(stf_ref)
    stb_ref[...] = jnp.zeros_like(stb_ref)
    t_idx = lax.broadcasted_iota(jnp.int32, (L, L), 0)
    s_idx = lax.broadcasted_iota(jnp.int32, (L, L), 1)
    causal = s_idx <= t_idx
    anti = s_idx >= t_idx
    ones = jnp.ones((L, V7X_LANES), BF16)

    def directions(c, m_f, m_b):
        cb = nc - 1 - c
        sf = pl.ds(pl.multiple_of(c * L, L), L)
        sb = pl.ds(pl.multiple_of(cb * L, L), L)
        rows_f = rows_ref[c]
        rows_b = rows_ref[cb]
        h_f, m_f = _mlstm_chunk(q_ref[sf, :], kt_ref[c], jnp.concatenate([v_ref[sf, :], ones], axis=1),
                                cols_ref[sf, 0:1], rows_f[1:2, :], rows_f[2:3, :], stf_ref, m_f, causal)
        h_b, m_b = _mlstm_chunk(q_ref[sb, :], kt_ref[cb], jnp.concatenate([v_ref[sb, :], ones], axis=1),
                                cols_ref[sb, 4:5], rows_b[5:6, :], rows_b[6:7, :], stb_ref, m_b, anti)
        return sf, sb, h_f, h_b, m_f, m_b

    def first_half(c, carry):
        sf, sb, h_f, h_b, m_f, m_b = directions(c, *carry)
        acc_ref[sf, :] = h_f
        acc_ref[sb, :] = h_b
        return m_f, m_b

    def finish(sl, h_new):
        hh = (acc_ref[sl, :] + h_new) * jax.nn.sigmoid(o_ref[sl, :].astype(F32))
        mu = jnp.mean(hh, axis=1, keepdims=True)
        cen = hh - mu
        var = jnp.mean(cen * cen, axis=1, keepdims=True)
        out_ref[sl, :] = (cen * lax.rsqrt(var + LN_EPS) * hg_ref[...]).astype(out_ref.dtype)

    def second_half(c, carry):
        sf, sb, h_f, h_b, m_f, m_b = directions(c, *carry)
        finish(sf, h_f)
        finish(sb, h_b)
        return m_f, m_b

    zero = jnp.zeros((1, 1), F32)
    carry = lax.fori_loop(0, nc // 2, first_half, (zero, zero))
    lax.fori_loop(nc // 2, nc, second_half, carry)


def mlstm(q, kt, z, rows, cols, head_g, *, batch, seq_len, name):
    t = q.shape[0]
    nh, dh = MLSTM_HEADS, MLSTM_HEAD_DIM
    nc = seq_len // CHUNK
    assert nc % 2 == 0
    single = dict(pipeline_mode=pl.Buffered(1))
    vmem = 4 * _nbytes((seq_len, dh), BF16) + _nbytes((seq_len, V7X_LANES), F32)
    vmem += 2 * _nbytes((seq_len, dh), BF16) + _nbytes((seq_len, dh), F32) + 2 * _nbytes((nc, 8, CHUNK), F32)
    vmem += 4 * _nbytes((dh, dh + V7X_LANES), F32)
    return pl.pallas_call(
        _mlstm_kernel,
        grid=(batch, nh),
        in_specs=[pl.BlockSpec((seq_len, dh), lambda b, h: (b, h), **single),
                  pl.BlockSpec((nc, dh, CHUNK), lambda b, h: (b, h, 0), **single),
                  pl.BlockSpec((seq_len, dh), lambda b, h: (b, 2 * nh + h), **single),
                  pl.BlockSpec((seq_len, dh), lambda b, h: (b, 3 * nh + h), **single),
                  pl.BlockSpec((None, nc, 8, CHUNK), lambda b, h: (h, b, 0, 0)),
                  pl.BlockSpec((None, None, seq_len, 8), lambda b, h: (b, h, 0, 0), **single),
                  pl.BlockSpec((None, 1, dh), lambda b, h: (h, 0, 0))],
        out_specs=pl.BlockSpec((seq_len, dh), lambda b, h: (b, h)),
        out_shape=jax.ShapeDtypeStruct((t, nh * dh), BF16),
        scratch_shapes=[pltpu.VMEM((seq_len, dh), F32),
                        pltpu.VMEM((dh, dh + V7X_LANES), F32), pltpu.VMEM((dh, dh + V7X_LANES), F32)],
        compiler_params=_cparams(2, vmem),
        name=name,
    )(q, kt, z, z, rows, cols, head_g.reshape(nh, 1, dh))


def _gelu_tanh(x):
    return 0.5 * x * (1.0 + jnp.tanh(math.sqrt(2.0 / math.pi) * (x + 0.044715 * (x * x * x))))


def _sgu_kernel(u_ref, v_ref, ng_ref, nb_ref, ws_ref, bs_ref, o_ref):
    n_chunks = u_ref.shape[0] // CHUNK
    ws = ws_ref[...]
    for c in range(n_chunks):
        sl = slice(c * CHUNK, (c + 1) * CHUNK)
        v = _gelu_tanh(v_ref[sl, :].astype(F32))
        mu = jnp.mean(v, axis=1, keepdims=True)
        cen = v - mu
        var = jnp.mean(cen * cen, axis=1, keepdims=True)
        vn = cen * lax.rsqrt(var + LN_EPS) * ng_ref[...] + nb_ref[...]
        sp = jnp.dot(ws, vn.astype(BF16), preferred_element_type=F32) + bs_ref[...]
        o_ref[sl, :] = (_gelu_tanh(u_ref[sl, :].astype(F32)) * sp).astype(o_ref.dtype)


def spatial_gating(z, norm_g, norm_b, w_s, b_s, *, u_col, v_col, tr, name):
    t = z.shape[0]
    g, dg = SGU_GROUPS, SGU_GROUP_DIM
    tr = min(tr, t)
    assert t % tr == 0 and tr % CHUNK == 0
    vmem = 2 * 3 * _nbytes((tr, dg), BF16) + 8 * _nbytes((CHUNK, dg), F32) * (tr // CHUNK)
    return pl.pallas_call(
        _sgu_kernel,
        grid=(g, t // tr),
        in_specs=[pl.BlockSpec((tr, dg), lambda j, i: (i, u_col + j)),
                  pl.BlockSpec((tr, dg), lambda j, i: (i, v_col + j)),
                  pl.BlockSpec((None, 1, dg), lambda j, i: (j, 0, 0)),
                  pl.BlockSpec((None, 1, dg), lambda j, i: (j, 0, 0)),
                  pl.BlockSpec((None, CHUNK, CHUNK), lambda j, i: (j, 0, 0)),
                  pl.BlockSpec((None, CHUNK, 1), lambda j, i: (j, 0, 0))],
        out_specs=pl.BlockSpec((tr, dg), lambda j, i: (i, j)),
        out_shape=jax.ShapeDtypeStruct((t, g * dg), BF16),
        compiler_params=_cparams(2, vmem),
        name=name,
    )(z, z, norm_g.reshape(g, 1, dg), norm_b.reshape(g, 1, dg), w_s.astype(BF16), b_s.reshape(g, CHUNK, 1))


def _router_kernel(x_ref, w_ref, b_ref, comb_ref):
    logits = jnp.dot(x_ref[...], w_ref[...], preferred_element_type=F32)
    lane = lax.broadcasted_iota(jnp.int32, logits.shape, 1)
    valid = lane < N_EXPERTS
    scores = jax.nn.sigmoid(logits)
    sel = jnp.where(valid, scores + b_ref[...], -jnp.inf)
    picked = jnp.zeros(logits.shape, jnp.bool_)
    for _ in range(TOP_K):
        best = jnp.max(sel, axis=1, keepdims=True)
        first = jnp.min(jnp.where(sel == best, lane, V7X_LANES), axis=1, keepdims=True)
        hit = lane == first
        picked = jnp.logical_or(picked, hit)
        sel = jnp.where(hit, -jnp.inf, sel)
    w = jnp.where(picked, scores, 0.0)
    comb_ref[...] = w / jnp.sum(w, axis=1, keepdims=True) * ROUTE_SCALE


def router(xb, w_r, b_r, *, tm, name):
    t, d = xb.shape
    tm = min(tm, t)
    assert t % tm == 0
    vmem = 2 * (_nbytes((tm, d), BF16) + _nbytes((d, V7X_LANES), BF16) + _nbytes((tm, V7X_LANES), F32))
    vmem += 8 * _nbytes((tm, V7X_LANES), F32)
    return pl.pallas_call(
        _router_kernel,
        grid=(t // tm,),
        in_specs=[pl.BlockSpec((tm, d), lambda i: (i, 0)),
                  pl.BlockSpec((d, V7X_LANES), lambda i: (0, 0)),
                  pl.BlockSpec((1, V7X_LANES), lambda i: (0, 0))],
        out_specs=pl.BlockSpec((tm, V7X_LANES), lambda i: (i, 0)),
        out_shape=jax.ShapeDtypeStruct((t, V7X_LANES), F32),
        compiler_params=_cparams(1, vmem),
        name=name,
    )(xb, w_r, b_r)


def _swiglu(x, wg, wu):
    gate = jnp.dot(x, wg, preferred_element_type=F32)
    up = jnp.dot(x, wu, preferred_element_type=F32)
    return gate * jax.nn.sigmoid(gate) * up


def _routed_up_kernel(x_ref, comb_ref, wg_ref, wu_ref, h_ref, wgc_ref, wuc_ref):
    per = wg_ref.shape[0]
    fe = wg_ref.shape[2]
    j = pl.program_id(0)

    @pl.when(pl.program_id(1) == 0)
    def _():
        for e in range(per):
            wgc_ref[:, e * fe:(e + 1) * fe] = wg_ref[e]
            wuc_ref[:, e * fe:(e + 1) * fe] = wu_ref[e]

    h = _swiglu(x_ref[...], wgc_ref[...], wuc_ref[...])
    e_row = lax.broadcasted_iota(jnp.int32, (V7X_LANES, per * fe), 0)
    e_col = lax.broadcasted_iota(jnp.int32, (V7X_LANES, per * fe), 1) // fe + j * per
    expand = jnp.where(e_row == e_col, 1.0, 0.0).astype(BF16)
    comb = comb_ref[...]
    c_hi = comb.astype(BF16)
    c_lo = (comb - c_hi.astype(F32)).astype(BF16)
    scale = jnp.dot(c_hi, expand, preferred_element_type=F32) + jnp.dot(c_lo, expand, preferred_element_type=F32)
    h_ref[...] = (h * scale).astype(h_ref.dtype)


def routed_up(xb, comb, wg, wu, *, layer, tm, per, name):
    t, d = xb.shape
    _, n_e, _, fe = wg.shape
    tm = min(tm, t)
    assert t % tm == 0 and n_e % per == 0
    tf = per * fe
    vmem = 2 * (_nbytes((tm, d), BF16) + _nbytes((tm, V7X_LANES), F32) + 2 * _nbytes((d, tf), BF16))
    vmem += 2 * _nbytes((tm, tf), BF16) + 2 * _nbytes((d, tf), BF16) + 5 * _nbytes((tm, tf), F32)
    return pl.pallas_call(
        _routed_up_kernel,
        grid=(n_e // per, t // tm),
        in_specs=[pl.BlockSpec((tm, d), lambda j, i: (i, 0)),
                  pl.BlockSpec((tm, V7X_LANES), lambda j, i: (i, 0)),
                  pl.BlockSpec((None, per, d, fe), lambda j, i: (layer, j, 0, 0)),
                  pl.BlockSpec((None, per, d, fe), lambda j, i: (layer, j, 0, 0))],
        out_specs=pl.BlockSpec((tm, tf), lambda j, i: (i, j)),
        out_shape=jax.ShapeDtypeStruct((t, n_e * fe), BF16),
        scratch_shapes=[pltpu.VMEM((d, tf), BF16), pltpu.VMEM((d, tf), BF16)],
        compiler_params=_cparams(2, vmem),
        name=name,
    )(xb, comb, wg, wu)


def _shared_up_kernel(x_ref, wg_ref, wu_ref, h_ref):
    h_ref[...] = _swiglu(x_ref[...], wg_ref[...], wu_ref[...]).astype(h_ref.dtype)


def shared_up(xb, wg, wu, *, layer, tm, tf, name):
    t, d = xb.shape
    f = wg.shape[2]
    tm, tf = min(tm, t), min(tf, f)
    assert t % tm == 0 and f % tf == 0
    vmem = 2 * (_nbytes((tm, d), BF16) + 2 * _nbytes((d, tf), BF16) + _nbytes((tm, tf), BF16))
    vmem += 4 * _nbytes((tm, tf), F32)
    return pl.pallas_call(
        _shared_up_kernel,
        grid=(f // tf, t // tm),
        in_specs=[pl.BlockSpec((tm, d), lambda j, i: (i, 0)),
                  pl.BlockSpec((None, d, tf), lambda j, i: (layer, 0, j)),
                  pl.BlockSpec((None, d, tf), lambda j, i: (layer, 0, j))],
        out_specs=pl.BlockSpec((tm, tf), lambda j, i: (i, j)),
        out_shape=jax.ShapeDtypeStruct((t, f), BF16),
        compiler_params=_cparams(2, vmem),
        name=name,
    )(xb, wg, wu)


def kernel(x, w_in, b_gates, conv_qk, mlstm_norm_g, sgu_norm_g, sgu_norm_b, w_spatial, b_spatial, w_out,
           ln1_g, ln1_b, w_router, b_router, w_exp_gate, w_exp_up, w_exp_down, w_sh_gate, w_sh_up, w_sh_down,
           ln2_g, ln2_b):
    bsz, seq_len, d = x.shape
    depth = w_in.shape[0]
    t = bsz * seq_len
    nh, dh = MLSTM_HEADS, MLSTM_HEAD_DIM
    w_m = nh * dh
    w_s = SGU_GROUPS * SGU_GROUP_DIM
    off_g = 4 * w_m
    n_e, f_e = w_exp_gate.shape[1], w_exp_gate.shape[3]
    nc_total = t // CHUNK

    xf = x.reshape(t, d)
    xb = xf.astype(BF16)
    lane_pad = V7X_LANES - N_EXPERTS

    w_usv = w_in[:, :, off_g + N_GATES:].astype(BF16)
    w_gate = jnp.pad(w_in[:, :, off_g:off_g + N_GATES], ((0, 0), (0, 0), (0, V7X_LANES - N_GATES))).astype(BF16)
    w_r = jnp.pad(w_router, ((0, 0), (0, 0), (0, lane_pad))).astype(BF16)
    b_r = jnp.pad(b_router, ((0, 0), (0, lane_pad))).reshape(depth, 1, V7X_LANES)
    wg_e, wu_e = w_exp_gate.astype(BF16), w_exp_up.astype(BF16)
    wd_e = w_exp_down.reshape(depth, n_e * f_e, d).astype(BF16)
    wg_s, wu_s, wd_s = w_sh_gate.astype(BF16), w_sh_up.astype(BF16), w_sh_down.astype(BF16)

    for l in range(depth):
        z_m = matmul(xb, w_in, tm=1024, tn=1024, out_dtype=BF16, n_cols=off_g, layer=l, name=f"in_proj_m_{l}")
        z_s = matmul(xb, w_usv, tm=1024, tn=1024, out_dtype=BF16, layer=l, name=f"in_proj_s_{l}")
        g_raw = matmul(xb, w_gate, tm=1024, tn=V7X_LANES, out_dtype=F32, layer=l, name=f"gate_proj_{l}")
        g4 = g_raw[:, :N_GATES].T.reshape(4, nh, t)
        tables = gate_prep(g4, b_gates[l].reshape(4, nh, 1), tl=2048, name=f"gate_prep_{l}")
        rows = tables.transpose(1, 0, 2).reshape(nh, 8, nc_total, CHUNK).transpose(0, 2, 1, 3)
        cols = tables.reshape(8, nh, bsz, seq_len).transpose(2, 1, 3, 0)
        q_c, kt_c = conv_silu(z_m, conv_qk[l], width=w_m, seq_len=seq_len, k_scale=dh ** -0.5, tr=512, tc=512,
                              name=f"conv_silu_{l}")
        h_m = mlstm(q_c, kt_c, z_m, rows, cols, mlstm_norm_g[l], batch=bsz, seq_len=seq_len, name=f"mlstm_{l}")
        h_s = spatial_gating(z_s, sgu_norm_g[l], sgu_norm_b[l], w_spatial[l], b_spatial[l],
                             u_col=0, v_col=w_s // SGU_GROUP_DIM, tr=1024, name=f"sgu_{l}")
        mix = matmul_cat(h_m, h_s, w_out, w_out, tm=1024, tn=1024, out_dtype=BF16, layer=l, row_blocks=(0, 1),
                         name=f"out_proj_{l}")
        x1, x1b = add_layer_norm(xf, mix, ln1_g[l], ln1_b[l], tm=256, name=f"ln1_{l}")

        comb = router(x1b, w_r[l], b_r[l], tm=512, name=f"router_{l}")
        hid_r = routed_up(x1b, comb, wg_e, wu_e, layer=l, tm=1024, per=4, name=f"routed_up_{l}")
        hid_s = shared_up(x1b, wg_s, wu_s, layer=l, tm=1024, tf=512, name=f"shared_up_{l}")
        y = matmul_cat(hid_r, hid_s, wd_e, wd_s, tm=512, tn=512, out_dtype=BF16, layer=l, name=f"ffn_down_{l}")
        xf, xb = add_layer_norm(x1, y, ln2_g[l], ln2_b[l], tm=256, name=f"ln2_{l}")

    return xf.reshape(bsz, seq_len, d)
```

```python
import functools
import math

import jax
import jax.numpy as jnp
from jax import lax
from jax.experimental import pallas as pl
from jax.experimental.pallas import tpu as pltpu

MLSTM_HEADS = 8
MLSTM_HEAD_DIM = 256
CHUNK = 128
SGU_GROUPS = 8
SGU_GROUP_DIM = 256
N_GATES = 4 * MLSTM_HEADS
N_EXPERTS = 64
TOP_K = 8
D_EXPERT = 128
ROUTE_SCALE = 2.5
DEPTH_FOR_NORM = 4
DEEPNORM_ALPHA = (2 * DEPTH_FOR_NORM) ** 0.25
LN_EPS = 1e-5

V7X_LANES = 128
V7X_VMEM_LIMIT_BYTES = 60000 * 1024

F32 = jnp.float32
BF16 = jnp.bfloat16


def _cparams(n_axes, vmem_bytes):
    limit = int(min(max(vmem_bytes * 5 // 4 + (4 << 20), 16 << 20), V7X_VMEM_LIMIT_BYTES))
    return pltpu.CompilerParams(dimension_semantics=("arbitrary",) * n_axes, vmem_limit_bytes=limit)


def _nbytes(shape, dtype):
    return math.prod(shape) * jnp.dtype(dtype).itemsize


def _weight_spec(w, rows, tn, layer, row_block=0, transposed=False, single=False):
    mode = dict(pipeline_mode=pl.Buffered(1)) if (single or w.dtype != BF16) else {}
    shape = (tn, rows) if transposed else (rows, tn)
    index = (lambda j: (j, row_block)) if transposed else (lambda j: (row_block, j))
    if w.ndim == 2:
        return pl.BlockSpec(shape, lambda j, i: index(j), **mode)
    return pl.BlockSpec((None,) + shape, lambda j, i: (layer,) + index(j), **mode)


_NT = (((1,), (1,)), ((), ()))


def _mm_kernel(a_ref, b_ref, o_ref, *scratch, rhs_t):
    if scratch:
        (wb_ref,) = scratch

        @pl.when(pl.program_id(1) == 0)
        def _():
            wb_ref[...] = b_ref[...].astype(BF16)

        b = wb_ref[...]
    else:
        b = b_ref[...]
    if rhs_t:
        acc = lax.dot_general(a_ref[...], b, _NT, preferred_element_type=F32)
    else:
        acc = jnp.dot(a_ref[...], b, preferred_element_type=F32)
    if len(o_ref.shape) == 2:
        o_ref[...] = acc.astype(o_ref.dtype)
    else:
        w = o_ref.shape[2]
        for c in range(o_ref.shape[0]):
            o_ref[c] = acc[:, c * w:(c + 1) * w].astype(o_ref.dtype)


def matmul(a, b, *, tm, tn, out_dtype, name, n_cols=None, layer=None, rhs_t=False, slab=None):
    m, k = a.shape
    n = (b.shape[-2] if rhs_t else b.shape[-1]) if n_cols is None else n_cols
    tm, tn = min(tm, m), min(tn, n)
    assert m % tm == 0 and n % tn == 0
    cast = b.dtype != BF16
    vmem = 2 * (_nbytes((tm, k), a.dtype) + _nbytes((tm, tn), out_dtype)) + _nbytes((tm, tn), F32)
    vmem += _nbytes((k, tn), F32) + _nbytes((k, tn), BF16) if cast else 2 * _nbytes((k, tn), BF16)
    if slab is None:
        out_spec = pl.BlockSpec((tm, tn), lambda j, i: (i, j))
        out_shape = jax.ShapeDtypeStruct((m, n), out_dtype)
    else:
        assert tn % slab == 0
        out_spec = pl.BlockSpec((tn // slab, tm, slab), lambda j, i: (j, i, 0))
        out_shape = jax.ShapeDtypeStruct((n // slab, m, slab), out_dtype)
    return pl.pallas_call(
        functools.partial(_mm_kernel, rhs_t=rhs_t),
        grid=(n // tn, m // tm),
        in_specs=[pl.BlockSpec((tm, k), lambda j, i: (i, 0)), _weight_spec(b, k, tn, layer, transposed=rhs_t)],
        out_specs=out_spec,
        out_shape=out_shape,
        scratch_shapes=[pltpu.VMEM((tn, k) if rhs_t else (k, tn), BF16)] if cast else [],
        compiler_params=_cparams(2, vmem),
        name=name,
    )(a, b)


def _mm2_kernel(a1_ref, a2_ref, b1_ref, b2_ref, o_ref, *scratch):
    if scratch:
        w1_ref, w2_ref = scratch

        @pl.when(pl.program_id(1) == 0)
        def _():
            w1_ref[...] = b1_ref[...].astype(BF16)
            w2_ref[...] = b2_ref[...].astype(BF16)

        b1, b2 = w1_ref[...], w2_ref[...]
    else:
        b1, b2 = b1_ref[...], b2_ref[...]
    acc = jnp.dot(a1_ref[...], b1, preferred_element_type=F32)
    acc = acc + jnp.dot(a2_ref[...], b2, preferred_element_type=F32)
    o_ref[...] = acc.astype(o_ref.dtype)


def matmul_cat(a1, a2, b1, b2, *, tm, tn, out_dtype, name, layer=None, row_blocks=(0, 0), single=False):
    m, k1 = a1.shape
    _, k2 = a2.shape
    n = b1.shape[-1]
    tm, tn = min(tm, m), min(tn, n)
    assert m % tm == 0 and n % tn == 0
    k = k1 + k2
    assert row_blocks[0] == row_blocks[1] or k1 == k2
    cast = b1.dtype != BF16
    vmem = 2 * (_nbytes((tm, k), a1.dtype) + _nbytes((tm, tn), out_dtype)) + _nbytes((tm, tn), F32)
    vmem += _nbytes((k, tn), F32) + _nbytes((k, tn), BF16) if cast else (1 if single else 2) * _nbytes((k, tn), BF16)
    return pl.pallas_call(
        _mm2_kernel,
        grid=(n // tn, m // tm),
        in_specs=[pl.BlockSpec((tm, k1), lambda j, i: (i, 0)),
                  pl.BlockSpec((tm, k2), lambda j, i: (i, 0)),
                  _weight_spec(b1, k1, tn, layer, row_blocks[0], single=single),
                  _weight_spec(b2, k2, tn, layer, row_blocks[1], single=single)],
        out_specs=pl.BlockSpec((tm, tn), lambda j, i: (i, j)),
        out_shape=jax.ShapeDtypeStruct((m, n), out_dtype),
        scratch_shapes=[pltpu.VMEM((k1, tn), BF16), pltpu.VMEM((k2, tn), BF16)] if cast else [],
        compiler_params=_cparams(2, vmem),
        name=name,
    )(a1, a2, b1, b2)


def _add_ln_kernel(x_ref, r_ref, g_ref, b_ref, y_ref, yb_ref):
    v = DEEPNORM_ALPHA * x_ref[...] + r_ref[...].astype(F32)
    mu = jnp.mean(v, axis=-1, keepdims=True)
    c = v - mu
    var = jnp.mean(c * c, axis=-1, keepdims=True)
    y = c * lax.rsqrt(var + LN_EPS) * g_ref[...] + b_ref[...]
    y_ref[...] = y
    yb_ref[...] = y.astype(BF16)


def add_layer_norm(x, r, g, b, *, tm, name):
    t, d = x.shape
    tm = min(tm, t)
    assert t % tm == 0
    vmem = 2 * (3 * _nbytes((tm, d), F32) + _nbytes((tm, d), BF16)) + 4 * _nbytes((tm, d), F32)
    row = lambda i: (i, 0)
    return pl.pallas_call(
        _add_ln_kernel,
        grid=(t // tm,),
        in_specs=[pl.BlockSpec((tm, d), row), pl.BlockSpec((tm, d), row),
                  pl.BlockSpec((1, d), lambda i: (0, 0)), pl.BlockSpec((1, d), lambda i: (0, 0))],
        out_specs=[pl.BlockSpec((tm, d), row), pl.BlockSpec((tm, d), row)],
        out_shape=[jax.ShapeDtypeStruct((t, d), F32), jax.ShapeDtypeStruct((t, d), BF16)],
        compiler_params=_cparams(1, vmem),
        name=name,
    )(x, r, g.reshape(1, d), b.reshape(1, d))


_HALO = 16


def _conv3_silu(z_ref, zp_ref, zn_ref, w_ref, seq_len):
    tr = z_ref.shape[0]
    x = z_ref[...].astype(F32)
    row = lax.broadcasted_iota(jnp.int32, x.shape, 0)
    t0 = pl.program_id(0) * tr
    prev_row = jnp.where(lax.rem(t0, seq_len) == 0, 0.0, zp_ref[_HALO - 1:_HALO, :].astype(F32))
    next_row = jnp.where(lax.rem(t0 + tr, seq_len) == 0, 0.0, zn_ref[0:1, :].astype(F32))
    x_prev = jnp.where(row == 0, prev_row, pltpu.roll(x, 1, axis=0))
    x_next = jnp.where(row == tr - 1, next_row, pltpu.roll(x, tr - 1, axis=0))
    y = w_ref[0:1, :] * x_prev + w_ref[1:2, :] * x + w_ref[2:3, :] * x_next
    return y * jax.nn.sigmoid(y)


def _conv_silu_kernel(q_ref, qp_ref, qn_ref, k_ref, kp_ref, kn_ref, wq_ref, wk_ref, oq_ref, okt_ref,
                      *, seq_len, k_scale):
    oq_ref[...] = _conv3_silu(q_ref, qp_ref, qn_ref, wq_ref, seq_len).astype(oq_ref.dtype)
    k = _conv3_silu(k_ref, kp_ref, kn_ref, wk_ref, seq_len) * k_scale
    for c in range(okt_ref.shape[0]):
        okt_ref[c] = k[c * CHUNK:(c + 1) * CHUNK, :].T.astype(okt_ref.dtype)


def conv_silu(z3, conv_w, *, n_slabs, seq_len, k_scale, tr, name):
    _, t, tc = z3.shape
    tr = min(tr, seq_len)
    assert seq_len % tr == 0 and tr % CHUNK == 0
    hb = tr // _HALO
    n_hb = t // _HALO
    vmem = 2 * (3 * _nbytes((tr, tc), BF16) + 4 * _nbytes((_HALO, tc), BF16) + _nbytes((tr, tc), BF16))
    vmem += 12 * _nbytes((tr, tc), F32)
    cur = lambda off: (lambda i, j: (j + off, i, 0))
    prev = lambda off: (lambda i, j: (j + off, jnp.maximum(i * hb - 1, 0), 0))
    nxt = lambda off: (lambda i, j: (j + off, jnp.minimum((i + 1) * hb, n_hb - 1), 0))
    return pl.pallas_call(
        functools.partial(_conv_silu_kernel, seq_len=seq_len, k_scale=k_scale),
        grid=(t // tr, n_slabs),
        in_specs=[pl.BlockSpec((None, tr, tc), cur(0)),
                  pl.BlockSpec((None, _HALO, tc), prev(0)),
                  pl.BlockSpec((None, _HALO, tc), nxt(0)),
                  pl.BlockSpec((None, tr, tc), cur(n_slabs)),
                  pl.BlockSpec((None, _HALO, tc), prev(n_slabs)),
                  pl.BlockSpec((None, _HALO, tc), nxt(n_slabs)),
                  pl.BlockSpec((3, tc), lambda i, j: (0, j)),
                  pl.BlockSpec((3, tc), lambda i, j: (0, j + n_slabs))],
        out_specs=[pl.BlockSpec((None, tr, tc), lambda i, j: (j, i, 0)),
                   pl.BlockSpec((tr // CHUNK, tc, CHUNK), lambda i, j: (i, j, 0))],
        out_shape=[jax.ShapeDtypeStruct((n_slabs, t, tc), BF16),
                   jax.ShapeDtypeStruct((t // CHUNK, n_slabs * tc, CHUNK), BF16)],
        compiler_params=_cparams(2, vmem),
        name=name,
    )(z3, z3, z3, z3, z3, z3, conv_w, conv_w)


def _chunk_scan(x, lane_in_chunk, *, suffix):
    n = x.shape[-1]
    k = 1
    while k < CHUNK:
        if suffix:
            shifted = pltpu.roll(x, n - k, axis=1)
            x = x + jnp.where(lane_in_chunk < CHUNK - k, shifted, 0.0)
        else:
            shifted = pltpu.roll(x, k, axis=1)
            x = x + jnp.where(lane_in_chunk >= k, shifted, 0.0)
        k *= 2
    return x


def _gate_prep_kernel(g_ref, bias_ref, o_ref):
    i_f = g_ref[0] + bias_ref[0]
    lf_f = jax.nn.log_sigmoid(g_ref[1] + bias_ref[1])
    i_b = g_ref[2] + bias_ref[2]
    lf_b = jax.nn.log_sigmoid(g_ref[3] + bias_ref[3])
    lane = lax.rem(lax.broadcasted_iota(jnp.int32, i_f.shape, 1), CHUNK)
    b_f = _chunk_scan(lf_f, lane, suffix=False)
    g_f = b_f + _chunk_scan(lf_f, lane, suffix=True) - lf_f
    b_b = _chunk_scan(lf_b, lane, suffix=True)
    g_b = b_b + _chunk_scan(lf_b, lane, suffix=False) - lf_b
    r_f = i_f - b_f
    r_b = i_b - b_b
    o_ref[0] = b_f
    o_ref[1] = r_f
    o_ref[2] = g_f + r_f
    o_ref[3] = g_f
    o_ref[4] = b_b
    o_ref[5] = r_b
    o_ref[6] = g_b + r_b
    o_ref[7] = g_b


def gate_prep(g4, bias, *, tl, name):
    _, h, t = g4.shape
    tl = min(tl, t)
    assert t % tl == 0 and tl % CHUNK == 0
    vmem = 2 * (_nbytes((4, h, tl), F32) + _nbytes((8, h, tl), F32)) + 16 * _nbytes((h, tl), F32)
    return pl.pallas_call(
        _gate_prep_kernel,
        grid=(t // tl,),
        in_specs=[pl.BlockSpec((4, h, tl), lambda i: (0, 0, i)),
                  pl.BlockSpec((4, h, 1), lambda i: (0, 0, 0))],
        out_specs=pl.BlockSpec((8, h, tl), lambda i: (0, 0, i)),
        out_shape=jax.ShapeDtypeStruct((8, h, t), F32),
        compiler_params=_cparams(1, vmem),
        name=name,
    )(g4, bias)


def _lanes(x, n):
    return x if n == V7X_LANES else jnp.concatenate([x] * (n // V7X_LANES), axis=1)


def _mlstm_chunk(q, kt, v_ext, b_col, r_row, a_row, st_ref, m11, mask):
    dv = v_ext.shape[1] - V7X_LANES
    b_wide = jnp.broadcast_to(b_col, (b_col.shape[0], V7X_LANES))
    dmat = jnp.where(mask, b_wide + r_row, -jnp.inf)
    inter = b_wide + m11
    m_t = jnp.maximum(inter, jnp.max(dmat, axis=1, keepdims=True))
    w_intra = jnp.exp(dmat - m_t)
    w_inter = jnp.exp(inter - m_t)
    s_qk = jnp.dot(q, kt, preferred_element_type=F32) * w_intra
    intra = jnp.dot(s_qk.astype(BF16), v_ext, preferred_element_type=F32)
    state = st_ref[...]
    carried = jnp.dot(q, state.astype(BF16), preferred_element_type=F32)
    num = intra[:, :dv] + _lanes(w_inter, dv) * carried[:, :dv]
    den = intra[:, dv:] + w_inter * carried[:, dv:]
    h = num / _lanes(jnp.maximum(jnp.abs(den), jnp.exp(-m_t)), dv)
    g11 = a_row[:, 0:1] - r_row[:, 0:1]
    m_new = jnp.maximum(g11 + m11, jnp.max(a_row, axis=1, keepdims=True))
    wa = jnp.exp(a_row - m_new)
    decay = jnp.exp(g11 + m11 - m_new)
    ktw = (kt.astype(F32) * wa).astype(BF16)
    st_ref[...] = decay * state + jnp.dot(ktw, v_ext, preferred_element_type=F32)
    return h, m_new


def _mlstm_kernel(q_ref, kt_ref, v_ref, o_ref, rows_ref, cols_ref, hg_ref, out_ref, acc_ref, stf_ref, stb_ref):
    nc = rows_ref.shape[0]
    L = CHUNK
    stf_ref[...] = jnp.zeros_like(stf_ref)
    stb_ref[...] = jnp.zeros_like(stb_ref)
    t_idx = lax.broadcasted_iota(jnp.int32, (L, L), 0)
    s_idx = lax.broadcasted_iota(jnp.int32, (L, L), 1)
    causal = s_idx <= t_idx
    anti = s_idx >= t_idx
    ones = jnp.ones((L, V7X_LANES), BF16)

    def directions(c, m_f, m_b):
        cb = nc - 1 - c
        sf = pl.ds(pl.multiple_of(c * L, L), L)
        sb = pl.ds(pl.multiple_of(cb * L, L), L)
        rows_f = rows_ref[c]
        rows_b = rows_ref[cb]
        h_f, m_f = _mlstm_chunk(q_ref[sf, :], kt_ref[c], jnp.concatenate([v_ref[sf, :], ones], axis=1),
                                cols_ref[sf, 0:1], rows_f[1:2, :], rows_f[2:3, :], stf_ref, m_f, causal)
        h_b, m_b = _mlstm_chunk(q_ref[sb, :], kt_ref[cb], jnp.concatenate([v_ref[sb, :], ones], axis=1),
                                cols_ref[sb, 4:5], rows_b[5:6, :], rows_b[6:7, :], stb_ref, m_b, anti)
        return sf, sb, h_f, h_b, m_f, m_b

    def first_half(c, carry):
        sf, sb, h_f, h_b, m_f, m_b = directions(c, *carry)
        acc_ref[sf, :] = h_f
        acc_ref[sb, :] = h_b
        return m_f, m_b

    def finish(sl, h_new):
        hh = (acc_ref[sl, :] + h_new) * jax.nn.sigmoid(o_ref[sl, :].astype(F32))
        mu = jnp.mean(hh, axis=1, keepdims=True)
        cen = hh - mu
        var = jnp.mean(cen * cen, axis=1, keepdims=True)
        out_ref[sl, :] = (cen * lax.rsqrt(var + LN_EPS) * hg_ref[...]).astype(out_ref.dtype)

    def second_half(c, carry):
        sf, sb, h_f, h_b, m_f, m_b = directions(c, *carry)
        finish(sf, h_f)
        finish(sb, h_b)
        return m_f, m_b

    zero = jnp.zeros((1, 1), F32)
    carry = lax.fori_loop(0, nc // 2, first_half, (zero, zero))
    lax.fori_loop(nc // 2, nc, second_half, carry)


def mlstm(q3, kt, z3, rows, cols, head_g, *, batch, seq_len, name):
    t = q3.shape[1]
    nh, dh = MLSTM_HEADS, MLSTM_HEAD_DIM
    nc = seq_len // CHUNK
    assert nc % 2 == 0
    single = dict(pipeline_mode=pl.Buffered(1))
    vmem = 4 * _nbytes((seq_len, dh), BF16) + _nbytes((seq_len, V7X_LANES), F32)
    vmem += 2 * _nbytes((seq_len, dh), BF16) + _nbytes((seq_len, dh), F32) + 2 * _nbytes((nc, 8, CHUNK), F32)
    vmem += 4 * _nbytes((dh, dh + V7X_LANES), F32)
    return pl.pallas_call(
        _mlstm_kernel,
        grid=(batch, nh),
        in_specs=[pl.BlockSpec((None, seq_len, dh), lambda b, h: (h, b, 0), **single),
                  pl.BlockSpec((nc, dh, CHUNK), lambda b, h: (b, h, 0), **single),
                  pl.BlockSpec((None, seq_len, dh), lambda b, h: (2 * nh + h, b, 0), **single),
                  pl.BlockSpec((None, seq_len, dh), lambda b, h: (3 * nh + h, b, 0), **single),
                  pl.BlockSpec((None, nc, 8, CHUNK), lambda b, h: (h, b, 0, 0)),
                  pl.BlockSpec((None, None, seq_len, 8), lambda b, h: (b, h, 0, 0), **single),
                  pl.BlockSpec((None, 1, dh), lambda b, h: (h, 0, 0))],
        out_specs=pl.BlockSpec((seq_len, dh), lambda b, h: (b, h)),
        out_shape=jax.ShapeDtypeStruct((t, nh * dh), BF16),
        scratch_shapes=[pltpu.VMEM((seq_len, dh), F32),
                        pltpu.VMEM((dh, dh + V7X_LANES), F32), pltpu.VMEM((dh, dh + V7X_LANES), F32)],
        compiler_params=_cparams(2, vmem),
        name=name,
    )(q3, kt, z3, z3, rows, cols, head_g.reshape(nh, 1, dh))


def _gelu_tanh(x):
    return 0.5 * x * (1.0 + jnp.tanh(math.sqrt(2.0 / math.pi) * (x + 0.044715 * (x * x * x))))


def _sgu_kernel(u_ref, v_ref, ng_ref, nb_ref, ws_ref, bs_ref, o_ref):
    n_chunks = u_ref.shape[0] // CHUNK
    ws = ws_ref[...]
    for c in range(n_chunks):
        sl = slice(c * CHUNK, (c + 1) * CHUNK)
        v = _gelu_tanh(v_ref[sl, :].astype(F32))
        mu = jnp.mean(v, axis=1, keepdims=True)
        cen = v - mu
        var = jnp.mean(cen * cen, axis=1, keepdims=True)
        vn = cen * lax.rsqrt(var + LN_EPS) * ng_ref[...] + nb_ref[...]
        sp = jnp.dot(ws, vn.astype(BF16), preferred_element_type=F32) + bs_ref[...]
        o_ref[sl, :] = (_gelu_tanh(u_ref[sl, :].astype(F32)) * sp).astype(o_ref.dtype)


def spatial_gating(z, norm_g, norm_b, w_s, b_s, *, u_col, v_col, tr, name):
    t = z.shape[0]
    g, dg = SGU_GROUPS, SGU_GROUP_DIM
    tr = min(tr, t)
    assert t % tr == 0 and tr % CHUNK == 0
    vmem = 2 * 3 * _nbytes((tr, dg), BF16) + 8 * _nbytes((CHUNK, dg), F32) * (tr // CHUNK)
    return pl.pallas_call(
        _sgu_kernel,
        grid=(g, t // tr),
        in_specs=[pl.BlockSpec((tr, dg), lambda j, i: (i, u_col + j)),
                  pl.BlockSpec((tr, dg), lambda j, i: (i, v_col + j)),
                  pl.BlockSpec((None, 1, dg), lambda j, i: (j, 0, 0)),
                  pl.BlockSpec((None, 1, dg), lambda j, i: (j, 0, 0)),
                  pl.BlockSpec((None, CHUNK, CHUNK), lambda j, i: (j, 0, 0)),
                  pl.BlockSpec((None, CHUNK, 1), lambda j, i: (j, 0, 0))],
        out_specs=pl.BlockSpec((tr, dg), lambda j, i: (i, j)),
        out_shape=jax.ShapeDtypeStruct((t, g * dg), BF16),
        compiler_params=_cparams(2, vmem),
        name=name,
    )(z, z, norm_g.reshape(g, 1, dg), norm_b.reshape(g, 1, dg), w_s.astype(BF16), b_s.reshape(g, CHUNK, 1))


def _router_kernel(x_ref, w_ref, b_ref, comb_ref):
    logits = jnp.dot(x_ref[...], w_ref[...], preferred_element_type=F32)
    lane = lax.broadcasted_iota(jnp.int32, logits.shape, 1)
    valid = lane < N_EXPERTS
    scores = jax.nn.sigmoid(logits)
    sel = jnp.where(valid, scores + b_ref[...], -jnp.inf)
    picked = jnp.zeros(logits.shape, jnp.bool_)
    for _ in range(TOP_K):
        best = jnp.max(sel, axis=1, keepdims=True)
        first = jnp.min(jnp.where(sel == best, lane, V7X_LANES), axis=1, keepdims=True)
        hit = lane == first
        picked = jnp.logical_or(picked, hit)
        sel = jnp.where(hit, -jnp.inf, sel)
    w = jnp.where(picked, scores, 0.0)
    comb_ref[...] = w / jnp.sum(w, axis=1, keepdims=True) * ROUTE_SCALE


def router(xb, w_r, b_r, *, tm, name):
    t, d = xb.shape
    tm = min(tm, t)
    assert t % tm == 0
    vmem = 2 * (_nbytes((tm, d), BF16) + _nbytes((d, V7X_LANES), BF16) + _nbytes((tm, V7X_LANES), F32))
    vmem += 8 * _nbytes((tm, V7X_LANES), F32)
    return pl.pallas_call(
        _router_kernel,
        grid=(t // tm,),
        in_specs=[pl.BlockSpec((tm, d), lambda i: (i, 0)),
                  pl.BlockSpec((d, V7X_LANES), lambda i: (0, 0)),
                  pl.BlockSpec((1, V7X_LANES), lambda i: (0, 0))],
        out_specs=pl.BlockSpec((tm, V7X_LANES), lambda i: (i, 0)),
        out_shape=jax.ShapeDtypeStruct((t, V7X_LANES), F32),
        compiler_params=_cparams(1, vmem),
        name=name,
    )(xb, w_r, b_r)


def _swiglu(x, wg, wu):
    gate = jnp.dot(x, wg, preferred_element_type=F32)
    up = jnp.dot(x, wu, preferred_element_type=F32)
    return gate * jax.nn.sigmoid(gate) * up


def _routed_up_kernel(x_ref, comb_ref, wg_ref, wu_ref, h_ref, wgc_ref, wuc_ref):
    per = wg_ref.shape[0]
    fe = wg_ref.shape[2]
    j = pl.program_id(0)

    @pl.when(pl.program_id(1) == 0)
    def _():
        for e in range(per):
            wgc_ref[:, e * fe:(e + 1) * fe] = wg_ref[e]
            wuc_ref[:, e * fe:(e + 1) * fe] = wu_ref[e]

    h = _swiglu(x_ref[...], wgc_ref[...], wuc_ref[...])
    e_row = lax.broadcasted_iota(jnp.int32, (V7X_LANES, per * fe), 0)
    e_col = lax.broadcasted_iota(jnp.int32, (V7X_LANES, per * fe), 1) // fe + j * per
    expand = jnp.where(e_row == e_col, 1.0, 0.0).astype(BF16)
    comb = comb_ref[...]
    c_hi = comb.astype(BF16)
    c_lo = (comb - c_hi.astype(F32)).astype(BF16)
    scale = jnp.dot(c_hi, expand, preferred_element_type=F32) + jnp.dot(c_lo, expand, preferred_element_type=F32)
    h_ref[...] = (h * scale).astype(h_ref.dtype)


def routed_up(xb, comb, wg, wu, *, layer, tm, per, name):
    t, d = xb.shape
    _, n_e, _, fe = wg.shape
    tm = min(tm, t)
    assert t % tm == 0 and n_e % per == 0
    tf = per * fe
    vmem = 2 * (_nbytes((tm, d), BF16) + _nbytes((tm, V7X_LANES), F32) + 2 * _nbytes((d, tf), BF16))
    vmem += 2 * _nbytes((tm, tf), BF16) + 2 * _nbytes((d, tf), BF16) + 5 * _nbytes((tm, tf), F32)
    return pl.pallas_call(
        _routed_up_kernel,
        grid=(n_e // per, t // tm),
        in_specs=[pl.BlockSpec((tm, d), lambda j, i: (i, 0)),
                  pl.BlockSpec((tm, V7X_LANES), lambda j, i: (i, 0)),
                  pl.BlockSpec((None, per, d, fe), lambda j, i: (layer, j, 0, 0)),
                  pl.BlockSpec((None, per, d, fe), lambda j, i: (layer, j, 0, 0))],
        out_specs=pl.BlockSpec((tm, tf), lambda j, i: (i, j)),
        out_shape=jax.ShapeDtypeStruct((t, n_e * fe), BF16),
        scratch_shapes=[pltpu.VMEM((d, tf), BF16), pltpu.VMEM((d, tf), BF16)],
        compiler_params=_cparams(2, vmem),
        name=name,
    )(xb, comb, wg, wu)


def _shared_up_kernel(x_ref, wg_ref, wu_ref, h_ref):
    h_ref[...] = _swiglu(x_ref[...], wg_ref[...], wu_ref[...]).astype(h_ref.dtype)


def shared_up(xb, wg, wu, *, layer, tm, tf, name):
    t, d = xb.shape
    f = wg.shape[2]
    tm, tf = min(tm, t), min(tf, f)
    assert t % tm == 0 and f % tf == 0
    vmem = 2 * (_nbytes((tm, d), BF16) + 2 * _nbytes((d, tf), BF16) + _nbytes((tm, tf), BF16))
    vmem += 4 * _nbytes((tm, tf), F32)
    return pl.pallas_call(
        _shared_up_kernel,
        grid=(f // tf, t // tm),
        in_specs=[pl.BlockSpec((tm, d), lambda j, i: (i, 0)),
                  pl.BlockSpec((None, d, tf), lambda j, i: (layer, 0, j)),
                  pl.BlockSpec((None, d, tf), lambda j, i: (layer, 0, j))],
        out_specs=pl.BlockSpec((tm, tf), lambda j, i: (i, j)),
        out_shape=jax.ShapeDtypeStruct((t, f), BF16),
        compiler_params=_cparams(2, vmem),
        name=name,
    )(xb, wg, wu)


def kernel(x, w_in, b_gates, conv_qk, mlstm_norm_g, sgu_norm_g, sgu_norm_b, w_spatial, b_spatial, w_out,
           ln1_g, ln1_b, w_router, b_router, w_exp_gate, w_exp_up, w_exp_down, w_sh_gate, w_sh_up, w_sh_down,
           ln2_g, ln2_b):
    bsz, seq_len, d = x.shape
    depth = w_in.shape[0]
    t = bsz * seq_len
    nh, dh = MLSTM_HEADS, MLSTM_HEAD_DIM
    w_m = nh * dh
    w_s = SGU_GROUPS * SGU_GROUP_DIM
    off_g = 4 * w_m
    n_e, f_e = w_exp_gate.shape[1], w_exp_gate.shape[3]
    nc_total = t // CHUNK

    xf = x.reshape(t, d)
    xb = xf.astype(BF16)
    lane_pad = V7X_LANES - N_EXPERTS

    w_in_t = jnp.swapaxes(w_in, 1, 2)
    w_usv_t = w_in_t[:, off_g + N_GATES:, :].astype(BF16)
    w_gate_t = jnp.pad(w_in_t[:, off_g:off_g + N_GATES, :], ((0, 0), (0, V7X_LANES - N_GATES), (0, 0))).astype(BF16)
    w_r = jnp.pad(w_router, ((0, 0), (0, 0), (0, lane_pad))).astype(BF16)
    b_r = jnp.pad(b_router, ((0, 0), (0, lane_pad))).reshape(depth, 1, V7X_LANES)
    wg_e, wu_e = w_exp_gate.astype(BF16), w_exp_up.astype(BF16)
    wd_e = w_exp_down.reshape(depth, n_e * f_e, d).astype(BF16)
    wg_s, wu_s, wd_s = w_sh_gate.astype(BF16), w_sh_up.astype(BF16), w_sh_down.astype(BF16)

    for l in range(depth):
        z_m = matmul(xb, w_in_t, tm=1024, tn=1024, out_dtype=BF16, n_cols=off_g, layer=l, rhs_t=True, slab=dh,
                     name=f"in_proj_m_{l}")
        z_s = matmul(xb, w_usv_t, tm=1024, tn=1024, out_dtype=BF16, layer=l, rhs_t=True, name=f"in_proj_s_{l}")
        g_raw = matmul(xb, w_gate_t, tm=1024, tn=V7X_LANES, out_dtype=F32, layer=l, rhs_t=True,
                       name=f"gate_proj_{l}")
        g4 = g_raw[:, :N_GATES].T.reshape(4, nh, t)
        tables = gate_prep(g4, b_gates[l].reshape(4, nh, 1), tl=2048, name=f"gate_prep_{l}")
        rows = tables.transpose(1, 0, 2).reshape(nh, 8, nc_total, CHUNK).transpose(0, 2, 1, 3)
        cols = tables.reshape(8, nh, bsz, seq_len).transpose(2, 1, 3, 0)
        q_c, kt_c = conv_silu(z_m, conv_qk[l], n_slabs=nh, seq_len=seq_len, k_scale=dh ** -0.5, tr=1024,
                              name=f"conv_silu_{l}")
        h_m = mlstm(q_c, kt_c, z_m, rows, cols, mlstm_norm_g[l], batch=bsz, seq_len=seq_len, name=f"mlstm_{l}")
        h_s = spatial_gating(z_s, sgu_norm_g[l], sgu_norm_b[l], w_spatial[l], b_spatial[l],
                             u_col=0, v_col=w_s // SGU_GROUP_DIM, tr=1024, name=f"sgu_{l}")
        mix = matmul_cat(h_m, h_s, w_out, w_out, tm=1024, tn=1024, out_dtype=BF16, layer=l, row_blocks=(0, 1),
                         name=f"out_proj_{l}")
        x1, x1b = add_layer_norm(xf, mix, ln1_g[l], ln1_b[l], tm=256, name=f"ln1_{l}")

        comb = router(x1b, w_r[l], b_r[l], tm=512, name=f"router_{l}")
        hid_r = routed_up(x1b, comb, wg_e, wu_e, layer=l, tm=1024, per=4, name=f"routed_up_{l}")
        hid_s = shared_up(x1b, wg_s, wu_s, layer=l, tm=1024, tf=512, name=f"shared_up_{l}")
        y = matmul_cat(hid_r, hid_s, wd_e, wd_s, tm=512, tn=1024, out_dtype=BF16, layer=l, single=True,
                       name=f"ffn_down_{l}")
        xf, xb = add_layer_norm(x1, y, ln2_g[l], ln2_b[l], tm=256, name=f"ln2_{l}")

    return xf.reshape(bsz, seq_len, d)
```

```python
import functools
import math

import jax
import jax.numpy as jnp
from jax import lax
from jax.experimental import pallas as pl
from jax.experimental.pallas import tpu as pltpu

MLSTM_HEADS = 8
MLSTM_HEAD_DIM = 256
CHUNK = 128
SGU_GROUPS = 8
SGU_GROUP_DIM = 256
N_GATES = 4 * MLSTM_HEADS
N_EXPERTS = 64
TOP_K = 8
D_EXPERT = 128
ROUTE_SCALE = 2.5
DEPTH_FOR_NORM = 4
DEEPNORM_ALPHA = (2 * DEPTH_FOR_NORM) ** 0.25
LN_EPS = 1e-5

V7X_LANES = 128
V7X_VMEM_LIMIT_BYTES = 60000 * 1024

F32 = jnp.float32
BF16 = jnp.bfloat16


def _cparams(n_axes, vmem_bytes):
    limit = int(min(max(vmem_bytes * 5 // 4 + (4 << 20), 16 << 20), V7X_VMEM_LIMIT_BYTES))
    return pltpu.CompilerParams(dimension_semantics=("arbitrary",) * n_axes, vmem_limit_bytes=limit)


def _nbytes(shape, dtype):
    return math.prod(shape) * jnp.dtype(dtype).itemsize


def _weight_spec(w, rows, tn, layer, row_block=0, transposed=False, single=False):
    mode = dict(pipeline_mode=pl.Buffered(1)) if (single or w.dtype != BF16) else {}
    shape = (tn, rows) if transposed else (rows, tn)
    index = (lambda j: (j, row_block)) if transposed else (lambda j: (row_block, j))
    if w.ndim == 2:
        return pl.BlockSpec(shape, lambda j, i: index(j), **mode)
    return pl.BlockSpec((None,) + shape, lambda j, i: (layer,) + index(j), **mode)


_NT = (((1,), (1,)), ((), ()))


def _mm_kernel(a_ref, b_ref, o_ref, *scratch, rhs_t):
    if scratch:
        (wb_ref,) = scratch

        @pl.when(pl.program_id(1) == 0)
        def _():
            wb_ref[...] = b_ref[...].astype(BF16)

        b = wb_ref[...]
    else:
        b = b_ref[...]
    if rhs_t:
        acc = lax.dot_general(a_ref[...], b, _NT, preferred_element_type=F32)
    else:
        acc = jnp.dot(a_ref[...], b, preferred_element_type=F32)
    if len(o_ref.shape) == 2:
        o_ref[...] = acc.astype(o_ref.dtype)
    else:
        w = o_ref.shape[2]
        for c in range(o_ref.shape[0]):
            o_ref[c] = acc[:, c * w:(c + 1) * w].astype(o_ref.dtype)


def matmul(a, b, *, tm, tn, out_dtype, name, n_cols=None, layer=None, rhs_t=False, slab=None):
    m, k = a.shape
    n = (b.shape[-2] if rhs_t else b.shape[-1]) if n_cols is None else n_cols
    tm, tn = min(tm, m), min(tn, n)
    assert m % tm == 0 and n % tn == 0
    cast = b.dtype != BF16
    vmem = 2 * (_nbytes((tm, k), a.dtype) + _nbytes((tm, tn), out_dtype)) + _nbytes((tm, tn), F32)
    vmem += _nbytes((k, tn), F32) + _nbytes((k, tn), BF16) if cast else 2 * _nbytes((k, tn), BF16)
    if slab is None:
        out_spec = pl.BlockSpec((tm, tn), lambda j, i: (i, j))
        out_shape = jax.ShapeDtypeStruct((m, n), out_dtype)
    else:
        assert tn % slab == 0
        out_spec = pl.BlockSpec((tn // slab, tm, slab), lambda j, i: (j, i, 0))
        out_shape = jax.ShapeDtypeStruct((n // slab, m, slab), out_dtype)
    return pl.pallas_call(
        functools.partial(_mm_kernel, rhs_t=rhs_t),
        grid=(n // tn, m // tm),
        in_specs=[pl.BlockSpec((tm, k), lambda j, i: (i, 0)), _weight_spec(b, k, tn, layer, transposed=rhs_t)],
        out_specs=out_spec,
        out_shape=out_shape,
        scratch_shapes=[pltpu.VMEM((tn, k) if rhs_t else (k, tn), BF16)] if cast else [],
        compiler_params=_cparams(2, vmem),
        name=name,
    )(a, b)


def _mm2_kernel(a1_ref, a2_ref, b1_ref, b2_ref, o_ref, *scratch):
    if scratch:
        w1_ref, w2_ref = scratch

        @pl.when(pl.program_id(1) == 0)
        def _():
            w1_ref[...] = b1_ref[...].astype(BF16)
            w2_ref[...] = b2_ref[...].astype(BF16)

        b1, b2 = w1_ref[...], w2_ref[...]
    else:
        b1, b2 = b1_ref[...], b2_ref[...]
    acc = jnp.dot(a1_ref[...], b1, preferred_element_type=F32)
    acc = acc + jnp.dot(a2_ref[...], b2, preferred_element_type=F32)
    o_ref[...] = acc.astype(o_ref.dtype)


def matmul_cat(a1, a2, b1, b2, *, tm, tn, out_dtype, name, layer=None, row_blocks=(0, 0), single=False):
    m, k1 = a1.shape
    _, k2 = a2.shape
    n = b1.shape[-1]
    tm, tn = min(tm, m), min(tn, n)
    assert m % tm == 0 and n % tn == 0
    k = k1 + k2
    assert row_blocks[0] == row_blocks[1] or k1 == k2
    cast = b1.dtype != BF16
    vmem = 2 * (_nbytes((tm, k), a1.dtype) + _nbytes((tm, tn), out_dtype)) + _nbytes((tm, tn), F32)
    vmem += _nbytes((k, tn), F32) + _nbytes((k, tn), BF16) if cast else (1 if single else 2) * _nbytes((k, tn), BF16)
    return pl.pallas_call(
        _mm2_kernel,
        grid=(n // tn, m // tm),
        in_specs=[pl.BlockSpec((tm, k1), lambda j, i: (i, 0)),
                  pl.BlockSpec((tm, k2), lambda j, i: (i, 0)),
                  _weight_spec(b1, k1, tn, layer, row_blocks[0], single=single),
                  _weight_spec(b2, k2, tn, layer, row_blocks[1], single=single)],
        out_specs=pl.BlockSpec((tm, tn), lambda j, i: (i, j)),
        out_shape=jax.ShapeDtypeStruct((m, n), out_dtype),
        scratch_shapes=[pltpu.VMEM((k1, tn), BF16), pltpu.VMEM((k2, tn), BF16)] if cast else [],
        compiler_params=_cparams(2, vmem),
        name=name,
    )(a1, a2, b1, b2)


def _add_ln_kernel(x_ref, r_ref, g_ref, b_ref, y_ref, *yb_ref):
    v = DEEPNORM_ALPHA * x_ref[...] + r_ref[...].astype(F32)
    mu = jnp.mean(v, axis=-1, keepdims=True)
    c = v - mu
    var = jnp.mean(c * c, axis=-1, keepdims=True)
    y = c * lax.rsqrt(var + LN_EPS) * g_ref[...] + b_ref[...]
    y_ref[...] = y
    for ref in yb_ref:
        ref[...] = y.astype(BF16)


def add_layer_norm(x, r, g, b, *, tm, name, bf16_copy=True):
    t, d = x.shape
    tm = min(tm, t)
    assert t % tm == 0
    vmem = 2 * (3 * _nbytes((tm, d), F32) + _nbytes((tm, d), BF16)) + 4 * _nbytes((tm, d), F32)
    row = lambda i: (i, 0)
    out = pl.pallas_call(
        _add_ln_kernel,
        grid=(t // tm,),
        in_specs=[pl.BlockSpec((tm, d), row), pl.BlockSpec((tm, d), row),
                  pl.BlockSpec((1, d), lambda i: (0, 0)), pl.BlockSpec((1, d), lambda i: (0, 0))],
        out_specs=[pl.BlockSpec((tm, d), row)] * (2 if bf16_copy else 1),
        out_shape=[jax.ShapeDtypeStruct((t, d), F32)] + ([jax.ShapeDtypeStruct((t, d), BF16)] if bf16_copy else []),
        compiler_params=_cparams(1, vmem),
        name=name,
    )(x, r, g.reshape(1, d), b.reshape(1, d))
    return (out[0], out[1]) if bf16_copy else (out[0], None)


_HALO = 16


def _conv3_silu(z_ref, zp_ref, zn_ref, w_ref, seq_len):
    tr = z_ref.shape[0]
    x = z_ref[...].astype(F32)
    row = lax.broadcasted_iota(jnp.int32, x.shape, 0)
    t0 = pl.program_id(0) * tr
    prev_row = jnp.where(lax.rem(t0, seq_len) == 0, 0.0, zp_ref[_HALO - 1:_HALO, :].astype(F32))
    next_row = jnp.where(lax.rem(t0 + tr, seq_len) == 0, 0.0, zn_ref[0:1, :].astype(F32))
    x_prev = jnp.where(row == 0, prev_row, pltpu.roll(x, 1, axis=0))
    x_next = jnp.where(row == tr - 1, next_row, pltpu.roll(x, tr - 1, axis=0))
    y = w_ref[0:1, :] * x_prev + w_ref[1:2, :] * x + w_ref[2:3, :] * x_next
    return y * jax.nn.sigmoid(y)


def _conv_silu_kernel(q_ref, qp_ref, qn_ref, k_ref, kp_ref, kn_ref, wq_ref, wk_ref, oq_ref, okt_ref,
                      *, seq_len, k_scale):
    oq_ref[...] = _conv3_silu(q_ref, qp_ref, qn_ref, wq_ref, seq_len).astype(oq_ref.dtype)
    k = _conv3_silu(k_ref, kp_ref, kn_ref, wk_ref, seq_len) * k_scale
    for c in range(okt_ref.shape[0]):
        okt_ref[c] = k[c * CHUNK:(c + 1) * CHUNK, :].T.astype(okt_ref.dtype)


def conv_silu(z3, conv_w, *, n_slabs, seq_len, k_scale, tr, name):
    _, t, tc = z3.shape
    tr = min(tr, seq_len)
    assert seq_len % tr == 0 and tr % CHUNK == 0
    hb = tr // _HALO
    n_hb = t // _HALO
    vmem = 2 * (3 * _nbytes((tr, tc), BF16) + 4 * _nbytes((_HALO, tc), BF16) + _nbytes((tr, tc), BF16))
    vmem += 12 * _nbytes((tr, tc), F32)
    cur = lambda off: (lambda i, j: (j + off, i, 0))
    prev = lambda off: (lambda i, j: (j + off, jnp.maximum(i * hb - 1, 0), 0))
    nxt = lambda off: (lambda i, j: (j + off, jnp.minimum((i + 1) * hb, n_hb - 1), 0))
    return pl.pallas_call(
        functools.partial(_conv_silu_kernel, seq_len=seq_len, k_scale=k_scale),
        grid=(t // tr, n_slabs),
        in_specs=[pl.BlockSpec((None, tr, tc), cur(0)),
                  pl.BlockSpec((None, _HALO, tc), prev(0)),
                  pl.BlockSpec((None, _HALO, tc), nxt(0)),
                  pl.BlockSpec((None, tr, tc), cur(n_slabs)),
                  pl.BlockSpec((None, _HALO, tc), prev(n_slabs)),
                  pl.BlockSpec((None, _HALO, tc), nxt(n_slabs)),
                  pl.BlockSpec((3, tc), lambda i, j: (0, j)),
                  pl.BlockSpec((3, tc), lambda i, j: (0, j + n_slabs))],
        out_specs=[pl.BlockSpec((None, tr, tc), lambda i, j: (j, i, 0)),
                   pl.BlockSpec((tr // CHUNK, tc, CHUNK), lambda i, j: (i, j, 0))],
        out_shape=[jax.ShapeDtypeStruct((n_slabs, t, tc), BF16),
                   jax.ShapeDtypeStruct((t // CHUNK, n_slabs * tc, CHUNK), BF16)],
        compiler_params=_cparams(2, vmem),
        name=name,
    )(z3, z3, z3, z3, z3, z3, conv_w, conv_w)


def _chunk_scan(x, lane_in_chunk, *, suffix):
    n = x.shape[-1]
    k = 1
    while k < CHUNK:
        if suffix:
            shifted = pltpu.roll(x, n - k, axis=1)
            x = x + jnp.where(lane_in_chunk < CHUNK - k, shifted, 0.0)
        else:
            shifted = pltpu.roll(x, k, axis=1)
            x = x + jnp.where(lane_in_chunk >= k, shifted, 0.0)
        k *= 2
    return x


def _gate_prep_kernel(g_ref, bias_ref, o_ref):
    i_f = g_ref[0] + bias_ref[0]
    lf_f = jax.nn.log_sigmoid(g_ref[1] + bias_ref[1])
    i_b = g_ref[2] + bias_ref[2]
    lf_b = jax.nn.log_sigmoid(g_ref[3] + bias_ref[3])
    lane = lax.rem(lax.broadcasted_iota(jnp.int32, i_f.shape, 1), CHUNK)
    b_f = _chunk_scan(lf_f, lane, suffix=False)
    g_f = b_f + _chunk_scan(lf_f, lane, suffix=True) - lf_f
    b_b = _chunk_scan(lf_b, lane, suffix=True)
    g_b = b_b + _chunk_scan(lf_b, lane, suffix=False) - lf_b
    r_f = i_f - b_f
    r_b = i_b - b_b
    o_ref[0] = b_f
    o_ref[1] = r_f
    o_ref[2] = g_f + r_f
    o_ref[3] = g_f
    o_ref[4] = b_b
    o_ref[5] = r_b
    o_ref[6] = g_b + r_b
    o_ref[7] = g_b


def gate_prep(g4, bias, *, tl, name):
    _, h, t = g4.shape
    tl = min(tl, t)
    assert t % tl == 0 and tl % CHUNK == 0
    vmem = 2 * (_nbytes((4, h, tl), F32) + _nbytes((8, h, tl), F32)) + 16 * _nbytes((h, tl), F32)
    return pl.pallas_call(
        _gate_prep_kernel,
        grid=(t // tl,),
        in_specs=[pl.BlockSpec((4, h, tl), lambda i: (0, 0, i)),
                  pl.BlockSpec((4, h, 1), lambda i: (0, 0, 0))],
        out_specs=pl.BlockSpec((8, h, tl), lambda i: (0, 0, i)),
        out_shape=jax.ShapeDtypeStruct((8, h, t), F32),
        compiler_params=_cparams(1, vmem),
        name=name,
    )(g4, bias)


def _lanes(x, n):
    return x if n == V7X_LANES else jnp.concatenate([x] * (n // V7X_LANES), axis=1)


def _mlstm_chunk(q, kt, v_ext, b_row, r_row, a_row, st_ref, m11, mask, eye):
    dv = v_ext.shape[1] - V7X_LANES
    b_col = jnp.sum(jnp.where(eye, b_row, 0.0), axis=1, keepdims=True)
    b_wide = jnp.broadcast_to(b_col, (b_col.shape[0], V7X_LANES))
    dmat = jnp.where(mask, b_wide + r_row, -jnp.inf)
    inter = b_wide + m11
    m_t = jnp.maximum(inter, jnp.max(dmat, axis=1, keepdims=True))
    w_intra = jnp.exp(dmat - m_t)
    w_inter = jnp.exp(inter - m_t)
    s_qk = jnp.dot(q, kt, preferred_element_type=F32) * w_intra
    intra = jnp.dot(s_qk.astype(BF16), v_ext, preferred_element_type=F32)
    state = st_ref[...]
    carried = jnp.dot(q, state.astype(BF16), preferred_element_type=F32)
    num = intra[:, :dv] + _lanes(w_inter, dv) * carried[:, :dv]
    den = intra[:, dv:] + w_inter * carried[:, dv:]
    h = num / _lanes(jnp.maximum(jnp.abs(den), jnp.exp(-m_t)), dv)
    g11 = a_row[:, 0:1] - r_row[:, 0:1]
    m_new = jnp.maximum(g11 + m11, jnp.max(a_row, axis=1, keepdims=True))
    wa = jnp.exp(a_row - m_new)
    decay = jnp.exp(g11 + m11 - m_new)
    ktw = (kt.astype(F32) * wa).astype(BF16)
    st_ref[...] = decay * state + jnp.dot(ktw, v_ext, preferred_element_type=F32)
    return h, m_new


def _mlstm_kernel(q_ref, kt_ref, v_ref, o_ref, rows_ref, hg_ref, out_ref, acc_ref, stf_ref, stb_ref):
    nc = rows_ref.shape[0]
    L = CHUNK
    stf_ref[...] = jnp.zeros_like(stf_ref)
    stb_ref[...] = jnp.zeros_like(stb_ref)
    t_idx = lax.broadcasted_iota(jnp.int32, (L, L), 0)
    s_idx = lax.broadcasted_iota(jnp.int32, (L, L), 1)
    causal = s_idx <= t_idx
    anti = s_idx >= t_idx
    eye = s_idx == t_idx
    ones = jnp.ones((L, V7X_LANES), BF16)

    def directions(c, m_f, m_b):
        cb = nc - 1 - c
        sf = pl.ds(pl.multiple_of(c * L, L), L)
        sb = pl.ds(pl.multiple_of(cb * L, L), L)
        rows_f = rows_ref[c]
        rows_b = rows_ref[cb]
        h_f, m_f = _mlstm_chunk(q_ref[sf, :], kt_ref[c], jnp.concatenate([v_ref[sf, :], ones], axis=1),
                                rows_f[0:1, :], rows_f[1:2, :], rows_f[2:3, :], stf_ref, m_f, causal, eye)
        h_b, m_b = _mlstm_chunk(q_ref[sb, :], kt_ref[cb], jnp.concatenate([v_ref[sb, :], ones], axis=1),
                                rows_b[4:5, :], rows_b[5:6, :], rows_b[6:7, :], stb_ref, m_b, anti, eye)
        return sf, sb, h_f, h_b, m_f, m_b

    def first_half(c, carry):
        sf, sb, h_f, h_b, m_f, m_b = directions(c, *carry)
        acc_ref[sf, :] = h_f
        acc_ref[sb, :] = h_b
        return m_f, m_b

    def finish(sl, h_new):
        hh = (acc_ref[sl, :] + h_new) * jax.nn.sigmoid(o_ref[sl, :].astype(F32))
        mu = jnp.mean(hh, axis=1, keepdims=True)
        cen = hh - mu
        var = jnp.mean(cen * cen, axis=1, keepdims=True)
        out_ref[sl, :] = (cen * lax.rsqrt(var + LN_EPS) * hg_ref[...]).astype(out_ref.dtype)

    def second_half(c, carry):
        sf, sb, h_f, h_b, m_f, m_b = directions(c, *carry)
        finish(sf, h_f)
        finish(sb, h_b)
        return m_f, m_b

    zero = jnp.zeros((1, 1), F32)
    carry = lax.fori_loop(0, nc // 2, first_half, (zero, zero))
    lax.fori_loop(nc // 2, nc, second_half, carry)


def mlstm(q3, kt, z3, rows, head_g, *, batch, seq_len, name):
    t = q3.shape[1]
    nh, dh = MLSTM_HEADS, MLSTM_HEAD_DIM
    nc = seq_len // CHUNK
    assert nc % 2 == 0
    vmem = 2 * 4 * _nbytes((seq_len, dh), BF16)
    vmem += 2 * _nbytes((seq_len, dh), BF16) + _nbytes((seq_len, dh), F32) + 2 * _nbytes((nc, 8, CHUNK), F32)
    vmem += 4 * _nbytes((dh, dh + V7X_LANES), F32)
    return pl.pallas_call(
        _mlstm_kernel,
        grid=(batch, nh),
        in_specs=[pl.BlockSpec((None, seq_len, dh), lambda b, h: (h, b, 0)),
                  pl.BlockSpec((nc, dh, CHUNK), lambda b, h: (b, h, 0)),
                  pl.BlockSpec((None, seq_len, dh), lambda b, h: (2 * nh + h, b, 0)),
                  pl.BlockSpec((None, seq_len, dh), lambda b, h: (3 * nh + h, b, 0)),
                  pl.BlockSpec((None, nc, 8, CHUNK), lambda b, h: (h, b, 0, 0)),
                  pl.BlockSpec((None, 1, dh), lambda b, h: (h, 0, 0))],
        out_specs=pl.BlockSpec((seq_len, dh), lambda b, h: (b, h)),
        out_shape=jax.ShapeDtypeStruct((t, nh * dh), BF16),
        scratch_shapes=[pltpu.VMEM((seq_len, dh), F32),
                        pltpu.VMEM((dh, dh + V7X_LANES), F32), pltpu.VMEM((dh, dh + V7X_LANES), F32)],
        compiler_params=_cparams(2, vmem),
        name=name,
    )(q3, kt, z3, z3, rows, head_g.reshape(nh, 1, dh))


def _gelu_tanh(x):
    return 0.5 * x * (1.0 + jnp.tanh(math.sqrt(2.0 / math.pi) * (x + 0.044715 * (x * x * x))))


def _sgu_kernel(u_ref, v_ref, ng_ref, nb_ref, ws_ref, bs_ref, o_ref):
    n_chunks = u_ref.shape[0] // CHUNK
    ws = ws_ref[...]
    for c in range(n_chunks):
        sl = slice(c * CHUNK, (c + 1) * CHUNK)
        v = _gelu_tanh(v_ref[sl, :].astype(F32))
        mu = jnp.mean(v, axis=1, keepdims=True)
        cen = v - mu
        var = jnp.mean(cen * cen, axis=1, keepdims=True)
        vn = cen * lax.rsqrt(var + LN_EPS) * ng_ref[...] + nb_ref[...]
        sp = jnp.dot(ws, vn.astype(BF16), preferred_element_type=F32) + bs_ref[...]
        o_ref[sl, :] = (_gelu_tanh(u_ref[sl, :].astype(F32)) * sp).astype(o_ref.dtype)


def spatial_gating(z, norm_g, norm_b, w_s, b_s, *, u_col, v_col, tr, name):
    t = z.shape[0]
    g, dg = SGU_GROUPS, SGU_GROUP_DIM
    tr = min(tr, t)
    assert t % tr == 0 and tr % CHUNK == 0
    vmem = 2 * 3 * _nbytes((tr, dg), BF16) + 8 * _nbytes((CHUNK, dg), F32) * (tr // CHUNK)
    return pl.pallas_call(
        _sgu_kernel,
        grid=(g, t // tr),
        in_specs=[pl.BlockSpec((tr, dg), lambda j, i: (i, u_col + j)),
                  pl.BlockSpec((tr, dg), lambda j, i: (i, v_col + j)),
                  pl.BlockSpec((None, 1, dg), lambda j, i: (j, 0, 0)),
                  pl.BlockSpec((None, 1, dg), lambda j, i: (j, 0, 0)),
                  pl.BlockSpec((None, CHUNK, CHUNK), lambda j, i: (j, 0, 0)),
                  pl.BlockSpec((None, CHUNK, 1), lambda j, i: (j, 0, 0))],
        out_specs=pl.BlockSpec((tr, dg), lambda j, i: (i, j)),
        out_shape=jax.ShapeDtypeStruct((t, g * dg), BF16),
        compiler_params=_cparams(2, vmem),
        name=name,
    )(z, z, norm_g.reshape(g, 1, dg), norm_b.reshape(g, 1, dg), w_s.astype(BF16), b_s.reshape(g, CHUNK, 1))


def _router_kernel(x_ref, w_ref, b_ref, comb_ref):
    logits = jnp.dot(x_ref[...], w_ref[...], preferred_element_type=F32)
    lane = lax.broadcasted_iota(jnp.int32, logits.shape, 1)
    valid = lane < N_EXPERTS
    scores = jax.nn.sigmoid(logits)
    sel = jnp.where(valid, scores + b_ref[...], -jnp.inf)
    picked = jnp.zeros(logits.shape, jnp.bool_)
    for _ in range(TOP_K):
        best = jnp.max(sel, axis=1, keepdims=True)
        first = jnp.min(jnp.where(sel == best, lane, V7X_LANES), axis=1, keepdims=True)
        hit = lane == first
        picked = jnp.logical_or(picked, hit)
        sel = jnp.where(hit, -jnp.inf, sel)
    w = jnp.where(picked, scores, 0.0)
    comb_ref[...] = w / jnp.sum(w, axis=1, keepdims=True) * ROUTE_SCALE


def router(xb, w_r, b_r, *, tm, name):
    t, d = xb.shape
    tm = min(tm, t)
    assert t % tm == 0
    vmem = 2 * (_nbytes((tm, d), BF16) + _nbytes((d, V7X_LANES), BF16) + _nbytes((tm, V7X_LANES), F32))
    vmem += 8 * _nbytes((tm, V7X_LANES), F32)
    return pl.pallas_call(
        _router_kernel,
        grid=(t // tm,),
        in_specs=[pl.BlockSpec((tm, d), lambda i: (i, 0)),
                  pl.BlockSpec((d, V7X_LANES), lambda i: (0, 0)),
                  pl.BlockSpec((1, V7X_LANES), lambda i: (0, 0))],
        out_specs=pl.BlockSpec((tm, V7X_LANES), lambda i: (i, 0)),
        out_shape=jax.ShapeDtypeStruct((t, V7X_LANES), F32),
        compiler_params=_cparams(1, vmem),
        name=name,
    )(xb, w_r, b_r)


def _swiglu(x, wg, wu):
    gate = jnp.dot(x, wg, preferred_element_type=F32)
    up = jnp.dot(x, wu, preferred_element_type=F32)
    return gate * jax.nn.sigmoid(gate) * up


def _routed_up_kernel(x_ref, comb_ref, wg_ref, wu_ref, h_ref, wgc_ref, wuc_ref):
    per = wg_ref.shape[0]
    fe = wg_ref.shape[2]
    j = pl.program_id(0)

    @pl.when(pl.program_id(1) == 0)
    def _():
        for e in range(per):
            wgc_ref[:, e * fe:(e + 1) * fe] = wg_ref[e]
            wuc_ref[:, e * fe:(e + 1) * fe] = wu_ref[e]

    h = _swiglu(x_ref[...], wgc_ref[...], wuc_ref[...])
    e_row = lax.broadcasted_iota(jnp.int32, (V7X_LANES, per * fe), 0)
    e_col = lax.broadcasted_iota(jnp.int32, (V7X_LANES, per * fe), 1) // fe + j * per
    expand = jnp.where(e_row == e_col, 1.0, 0.0).astype(BF16)
    comb = comb_ref[...]
    c_hi = comb.astype(BF16)
    c_lo = (comb - c_hi.astype(F32)).astype(BF16)
    scale = jnp.dot(c_hi, expand, preferred_element_type=F32) + jnp.dot(c_lo, expand, preferred_element_type=F32)
    h_ref[...] = (h * scale).astype(h_ref.dtype)


def routed_up(xb, comb, wg, wu, *, layer, tm, per, name):
    t, d = xb.shape
    _, n_e, _, fe = wg.shape
    tm = min(tm, t)
    assert t % tm == 0 and n_e % per == 0
    tf = per * fe
    vmem = 2 * (_nbytes((tm, d), BF16) + _nbytes((tm, V7X_LANES), F32) + 2 * _nbytes((d, tf), BF16))
    vmem += 2 * _nbytes((tm, tf), BF16) + 2 * _nbytes((d, tf), BF16) + 5 * _nbytes((tm, tf), F32)
    return pl.pallas_call(
        _routed_up_kernel,
        grid=(n_e // per, t // tm),
        in_specs=[pl.BlockSpec((tm, d), lambda j, i: (i, 0)),
                  pl.BlockSpec((tm, V7X_LANES), lambda j, i: (i, 0)),
                  pl.BlockSpec((None, per, d, fe), lambda j, i: (layer, j, 0, 0)),
                  pl.BlockSpec((None, per, d, fe), lambda j, i: (layer, j, 0, 0))],
        out_specs=pl.BlockSpec((tm, tf), lambda j, i: (i, j)),
        out_shape=jax.ShapeDtypeStruct((t, n_e * fe), BF16),
        scratch_shapes=[pltpu.VMEM((d, tf), BF16), pltpu.VMEM((d, tf), BF16)],
        compiler_params=_cparams(2, vmem),
        name=name,
    )(xb, comb, wg, wu)


def _shared_up_kernel(x_ref, wg_ref, wu_ref, h_ref):
    h_ref[...] = _swiglu(x_ref[...], wg_ref[...], wu_ref[...]).astype(h_ref.dtype)


def shared_up(xb, wg, wu, *, layer, tm, tf, name):
    t, d = xb.shape
    f = wg.shape[2]
    tm, tf = min(tm, t), min(tf, f)
    assert t % tm == 0 and f % tf == 0
    vmem = 2 * (_nbytes((tm, d), BF16) + 2 * _nbytes((d, tf), BF16) + _nbytes((tm, tf), BF16))
    vmem += 4 * _nbytes((tm, tf), F32)
    return pl.pallas_call(
        _shared_up_kernel,
        grid=(f // tf, t // tm),
        in_specs=[pl.BlockSpec((tm, d), lambda j, i: (i, 0)),
                  pl.BlockSpec((None, d, tf), lambda j, i: (layer, 0, j)),
                  pl.BlockSpec((None, d, tf), lambda j, i: (layer, 0, j))],
        out_specs=pl.BlockSpec((tm, tf), lambda j, i: (i, j)),
        out_shape=jax.ShapeDtypeStruct((t, f), BF16),
        compiler_params=_cparams(2, vmem),
        name=name,
    )(xb, wg, wu)


def kernel(x, w_in, b_gates, conv_qk, mlstm_norm_g, sgu_norm_g, sgu_norm_b, w_spatial, b_spatial, w_out,
           ln1_g, ln1_b, w_router, b_router, w_exp_gate, w_exp_up, w_exp_down, w_sh_gate, w_sh_up, w_sh_down,
           ln2_g, ln2_b):
    bsz, seq_len, d = x.shape
    depth = w_in.shape[0]
    t = bsz * seq_len
    nh, dh = MLSTM_HEADS, MLSTM_HEAD_DIM
    w_m = nh * dh
    w_s = SGU_GROUPS * SGU_GROUP_DIM
    off_g = 4 * w_m
    n_e, f_e = w_exp_gate.shape[1], w_exp_gate.shape[3]
    nc_total = t // CHUNK

    xf = x.reshape(t, d)
    xb = xf.astype(BF16)
    lane_pad = V7X_LANES - N_EXPERTS

    w_in_t = jnp.swapaxes(w_in, 1, 2)
    w_usv_t = w_in_t[:, off_g + N_GATES:, :].astype(BF16)
    w_gate_t = jnp.pad(w_in_t[:, off_g:off_g + N_GATES, :], ((0, 0), (0, V7X_LANES - N_GATES), (0, 0))).astype(BF16)
    w_r = jnp.pad(w_router, ((0, 0), (0, 0), (0, lane_pad))).astype(BF16)
    b_r = jnp.pad(b_router, ((0, 0), (0, lane_pad))).reshape(depth, 1, V7X_LANES)
    wg_e, wu_e = w_exp_gate.astype(BF16), w_exp_up.astype(BF16)
    wd_e = w_exp_down.reshape(depth, n_e * f_e, d).astype(BF16)
    wg_s, wu_s, wd_s = w_sh_gate.astype(BF16), w_sh_up.astype(BF16), w_sh_down.astype(BF16)

    for l in range(depth):
        z_m = matmul(xb, w_in_t, tm=1024, tn=1024, out_dtype=BF16, n_cols=off_g, layer=l, rhs_t=True, slab=dh,
                     name=f"in_proj_m_{l}")
        z_s = matmul(xb, w_usv_t, tm=1024, tn=1024, out_dtype=BF16, layer=l, rhs_t=True, name=f"in_proj_s_{l}")
        g_raw = matmul(xb, w_gate_t, tm=1024, tn=V7X_LANES, out_dtype=F32, layer=l, rhs_t=True,
                       name=f"gate_proj_{l}")
        g4 = g_raw[:, :N_GATES].T.reshape(4, nh, t)
        tables = gate_prep(g4, b_gates[l].reshape(4, nh, 1), tl=2048, name=f"gate_prep_{l}")
        rows = tables.transpose(1, 0, 2).reshape(nh, 8, nc_total, CHUNK).transpose(0, 2, 1, 3)
        q_c, kt_c = conv_silu(z_m, conv_qk[l], n_slabs=nh, seq_len=seq_len, k_scale=dh ** -0.5, tr=1024,
                              name=f"conv_silu_{l}")
        h_m = mlstm(q_c, kt_c, z_m, rows, mlstm_norm_g[l], batch=bsz, seq_len=seq_len, name=f"mlstm_{l}")
        h_s = spatial_gating(z_s, sgu_norm_g[l], sgu_norm_b[l], w_spatial[l], b_spatial[l],
                             u_col=0, v_col=w_s // SGU_GROUP_DIM, tr=1024, name=f"sgu_{l}")
        mix = matmul_cat(h_m, h_s, w_out, w_out, tm=1024, tn=1024, out_dtype=BF16, layer=l, row_blocks=(0, 1),
                         name=f"out_proj_{l}")
        x1, x1b = add_layer_norm(xf, mix, ln1_g[l], ln1_b[l], tm=256, name=f"ln1_{l}")

        comb = router(x1b, w_r[l], b_r[l], tm=512, name=f"router_{l}")
        hid_r = routed_up(x1b, comb, wg_e, wu_e, layer=l, tm=1024, per=4, name=f"routed_up_{l}")
        hid_s = shared_up(x1b, wg_s, wu_s, layer=l, tm=1024, tf=512, name=f"shared_up_{l}")
        y = matmul_cat(hid_r, hid_s, wd_e, wd_s, tm=512, tn=1024, out_dtype=BF16, layer=l, single=True,
                       name=f"ffn_down_{l}")
        xf, xb = add_layer_norm(x1, y, ln2_g[l], ln2_b[l], tm=256, bf16_copy=l + 1 < depth, name=f"ln2_{l}")

    return xf.reshape(bsz, seq_len, d)
```

```python
import functools
import math

import jax
import jax.numpy as jnp
from jax import lax
from jax.experimental import pallas as pl
from jax.experimental.pallas import tpu as pltpu

MLSTM_HEADS = 8
MLSTM_HEAD_DIM = 256
CHUNK = 128
SGU_GROUPS = 8
SGU_GROUP_DIM = 256
N_GATES = 4 * MLSTM_HEADS
N_EXPERTS = 64
TOP_K = 8
D_EXPERT = 128
ROUTE_SCALE = 2.5
DEPTH_FOR_NORM = 4
DEEPNORM_ALPHA = (2 * DEPTH_FOR_NORM) ** 0.25
LN_EPS = 1e-5

V7X_LANES = 128
V7X_VMEM_LIMIT_BYTES = 60000 * 1024

F32 = jnp.float32
BF16 = jnp.bfloat16


def _cparams(n_axes, vmem_bytes):
    limit = int(min(max(vmem_bytes * 5 // 4 + (4 << 20), 16 << 20), V7X_VMEM_LIMIT_BYTES))
    return pltpu.CompilerParams(dimension_semantics=("arbitrary",) * n_axes, vmem_limit_bytes=limit)


def _nbytes(shape, dtype):
    return math.prod(shape) * jnp.dtype(dtype).itemsize


def _weight_spec(w, rows, tn, layer, row_block=0, transposed=False, single=False):
    mode = dict(pipeline_mode=pl.Buffered(1)) if (single or w.dtype != BF16) else {}
    shape = (tn, rows) if transposed else (rows, tn)
    index = (lambda j: (j, row_block)) if transposed else (lambda j: (row_block, j))
    if w.ndim == 2:
        return pl.BlockSpec(shape, lambda j, i: index(j), **mode)
    return pl.BlockSpec((None,) + shape, lambda j, i: (layer,) + index(j), **mode)


_NT = (((1,), (1,)), ((), ()))


def _mm_kernel(a_ref, b_ref, o_ref, *scratch, rhs_t):
    if scratch:
        (wb_ref,) = scratch

        @pl.when(pl.program_id(1) == 0)
        def _():
            wb_ref[...] = b_ref[...].astype(BF16)

        b = wb_ref[...]
    else:
        b = b_ref[...]
    if rhs_t:
        acc = lax.dot_general(a_ref[...], b, _NT, preferred_element_type=F32)
    else:
        acc = jnp.dot(a_ref[...], b, preferred_element_type=F32)
    if len(o_ref.shape) == 2:
        o_ref[...] = acc.astype(o_ref.dtype)
    else:
        w = o_ref.shape[2]
        for c in range(o_ref.shape[0]):
            o_ref[c] = acc[:, c * w:(c + 1) * w].astype(o_ref.dtype)


def matmul(a, b, *, tm, tn, out_dtype, name, n_cols=None, layer=None, rhs_t=False, slab=None):
    m, k = a.shape
    n = (b.shape[-2] if rhs_t else b.shape[-1]) if n_cols is None else n_cols
    tm, tn = min(tm, m), min(tn, n)
    assert m % tm == 0 and n % tn == 0
    cast = b.dtype != BF16
    vmem = 2 * (_nbytes((tm, k), a.dtype) + _nbytes((tm, tn), out_dtype)) + _nbytes((tm, tn), F32)
    vmem += _nbytes((k, tn), F32) + _nbytes((k, tn), BF16) if cast else 2 * _nbytes((k, tn), BF16)
    if slab is None:
        out_spec = pl.BlockSpec((tm, tn), lambda j, i: (i, j))
        out_shape = jax.ShapeDtypeStruct((m, n), out_dtype)
    else:
        assert tn % slab == 0
        out_spec = pl.BlockSpec((tn // slab, tm, slab), lambda j, i: (j, i, 0))
        out_shape = jax.ShapeDtypeStruct((n // slab, m, slab), out_dtype)
    return pl.pallas_call(
        functools.partial(_mm_kernel, rhs_t=rhs_t),
        grid=(n // tn, m // tm),
        in_specs=[pl.BlockSpec((tm, k), lambda j, i: (i, 0)), _weight_spec(b, k, tn, layer, transposed=rhs_t)],
        out_specs=out_spec,
        out_shape=out_shape,
        scratch_shapes=[pltpu.VMEM((tn, k) if rhs_t else (k, tn), BF16)] if cast else [],
        compiler_params=_cparams(2, vmem),
        name=name,
    )(a, b)


def _mm2_kernel(a1_ref, a2_ref, b1_ref, b2_ref, o_ref, *scratch):
    if scratch:
        w1_ref, w2_ref = scratch

        @pl.when(pl.program_id(1) == 0)
        def _():
            w1_ref[...] = b1_ref[...].astype(BF16)
            w2_ref[...] = b2_ref[...].astype(BF16)

        b1, b2 = w1_ref[...], w2_ref[...]
    else:
        b1, b2 = b1_ref[...], b2_ref[...]
    acc = jnp.dot(a1_ref[...], b1, preferred_element_type=F32)
    acc = acc + jnp.dot(a2_ref[...], b2, preferred_element_type=F32)
    o_ref[...] = acc.astype(o_ref.dtype)


def matmul_cat(a1, a2, b1, b2, *, tm, tn, out_dtype, name, layer=None, row_blocks=(0, 0), single=False):
    m, k1 = a1.shape
    _, k2 = a2.shape
    n = b1.shape[-1]
    tm, tn = min(tm, m), min(tn, n)
    assert m % tm == 0 and n % tn == 0
    k = k1 + k2
    assert row_blocks[0] == row_blocks[1] or k1 == k2
    cast = b1.dtype != BF16
    vmem = 2 * (_nbytes((tm, k), a1.dtype) + _nbytes((tm, tn), out_dtype)) + _nbytes((tm, tn), F32)
    vmem += _nbytes((k, tn), F32) + _nbytes((k, tn), BF16) if cast else (1 if single else 2) * _nbytes((k, tn), BF16)
    return pl.pallas_call(
        _mm2_kernel,
        grid=(n // tn, m // tm),
        in_specs=[pl.BlockSpec((tm, k1), lambda j, i: (i, 0)),
                  pl.BlockSpec((tm, k2), lambda j, i: (i, 0)),
                  _weight_spec(b1, k1, tn, layer, row_blocks[0], single=single),
                  _weight_spec(b2, k2, tn, layer, row_blocks[1], single=single)],
        out_specs=pl.BlockSpec((tm, tn), lambda j, i: (i, j)),
        out_shape=jax.ShapeDtypeStruct((m, n), out_dtype),
        scratch_shapes=[pltpu.VMEM((k1, tn), BF16), pltpu.VMEM((k2, tn), BF16)] if cast else [],
        compiler_params=_cparams(2, vmem),
        name=name,
    )(a1, a2, b1, b2)


def _top_k_gate(logits, bias):
    lane = lax.broadcasted_iota(jnp.int32, logits.shape, 1)
    scores = jax.nn.sigmoid(logits)
    sel = jnp.where(lane < N_EXPERTS, scores + bias, -jnp.inf)
    picked = jnp.zeros(logits.shape, jnp.bool_)
    for _ in range(TOP_K):
        best = jnp.max(sel, axis=1, keepdims=True)
        first = jnp.min(jnp.where(sel == best, lane, V7X_LANES), axis=1, keepdims=True)
        hit = lane == first
        picked = jnp.logical_or(picked, hit)
        sel = jnp.where(hit, -jnp.inf, sel)
    w = jnp.where(picked, scores, 0.0)
    return w / jnp.sum(w, axis=1, keepdims=True) * ROUTE_SCALE


def _add_ln_kernel(x_ref, r_ref, g_ref, b_ref, *rest, bf16_copy, route):
    if route:
        wr_ref, br_ref, *outs = rest
    else:
        outs = rest
    v = DEEPNORM_ALPHA * x_ref[...] + r_ref[...].astype(F32)
    mu = jnp.mean(v, axis=-1, keepdims=True)
    c = v - mu
    var = jnp.mean(c * c, axis=-1, keepdims=True)
    y = c * lax.rsqrt(var + LN_EPS) * g_ref[...] + b_ref[...]
    outs[0][...] = y
    if bf16_copy:
        yb = y.astype(BF16)
        outs[1][...] = yb
        if route:
            logits = jnp.dot(yb, wr_ref[...], preferred_element_type=F32)
            outs[2][...] = _top_k_gate(logits, br_ref[...])


def add_layer_norm(x, r, g, b, *, tm, name, bf16_copy=True, router=None):
    t, d = x.shape
    tm = min(tm, t)
    assert t % tm == 0 and (router is None or bf16_copy)
    vmem = 2 * (3 * _nbytes((tm, d), F32) + _nbytes((tm, d), BF16)) + 4 * _nbytes((tm, d), F32)
    row = lambda i: (i, 0)
    fixed = lambda i: (0, 0)
    in_specs = [pl.BlockSpec((tm, d), row), pl.BlockSpec((tm, d), row),
                pl.BlockSpec((1, d), fixed), pl.BlockSpec((1, d), fixed)]
    operands = [x, r, g.reshape(1, d), b.reshape(1, d)]
    out_specs = [pl.BlockSpec((tm, d), row)]
    out_shape = [jax.ShapeDtypeStruct((t, d), F32)]
    if bf16_copy:
        out_specs.append(pl.BlockSpec((tm, d), row))
        out_shape.append(jax.ShapeDtypeStruct((t, d), BF16))
    if router is not None:
        in_specs += [pl.BlockSpec((d, V7X_LANES), fixed), pl.BlockSpec((1, V7X_LANES), fixed)]
        operands += list(router)
        out_specs.append(pl.BlockSpec((tm, V7X_LANES), row))
        out_shape.append(jax.ShapeDtypeStruct((t, V7X_LANES), F32))
        vmem += 2 * _nbytes((d, V7X_LANES), BF16) + 10 * _nbytes((tm, V7X_LANES), F32)
    out = pl.pallas_call(
        functools.partial(_add_ln_kernel, bf16_copy=bf16_copy, route=router is not None),
        grid=(t // tm,),
        in_specs=in_specs,
        out_specs=out_specs,
        out_shape=out_shape,
        compiler_params=_cparams(1, vmem),
        name=name,
    )(*operands)
    return out[0], (out[1] if bf16_copy else None), (out[2] if router is not None else None)


_HALO = 16


def _conv3_silu(z_ref, zp_ref, zn_ref, w_ref, seq_len):
    tr = z_ref.shape[0]
    x = z_ref[...].astype(F32)
    row = lax.broadcasted_iota(jnp.int32, x.shape, 0)
    t0 = pl.program_id(0) * tr
    prev_row = jnp.where(lax.rem(t0, seq_len) == 0, 0.0, zp_ref[_HALO - 1:_HALO, :].astype(F32))
    next_row = jnp.where(lax.rem(t0 + tr, seq_len) == 0, 0.0, zn_ref[0:1, :].astype(F32))
    x_prev = jnp.where(row == 0, prev_row, pltpu.roll(x, 1, axis=0))
    x_next = jnp.where(row == tr - 1, next_row, pltpu.roll(x, tr - 1, axis=0))
    y = w_ref[0:1, :] * x_prev + w_ref[1:2, :] * x + w_ref[2:3, :] * x_next
    return y * jax.nn.sigmoid(y)


def _conv_silu_kernel(q_ref, qp_ref, qn_ref, k_ref, kp_ref, kn_ref, wq_ref, wk_ref, oq_ref, okt_ref,
                      *, seq_len, k_scale):
    oq_ref[...] = _conv3_silu(q_ref, qp_ref, qn_ref, wq_ref, seq_len).astype(oq_ref.dtype)
    k = _conv3_silu(k_ref, kp_ref, kn_ref, wk_ref, seq_len) * k_scale
    for c in range(okt_ref.shape[0]):
        okt_ref[c] = k[c * CHUNK:(c + 1) * CHUNK, :].T.astype(okt_ref.dtype)


def conv_silu(z3, conv_w, *, n_slabs, seq_len, k_scale, tr, name):
    _, t, tc = z3.shape
    tr = min(tr, seq_len)
    assert seq_len % tr == 0 and tr % CHUNK == 0
    hb = tr // _HALO
    n_hb = t // _HALO
    vmem = 2 * (3 * _nbytes((tr, tc), BF16) + 4 * _nbytes((_HALO, tc), BF16) + _nbytes((tr, tc), BF16))
    vmem += 12 * _nbytes((tr, tc), F32)
    cur = lambda off: (lambda i, j: (j + off, i, 0))
    prev = lambda off: (lambda i, j: (j + off, jnp.maximum(i * hb - 1, 0), 0))
    nxt = lambda off: (lambda i, j: (j + off, jnp.minimum((i + 1) * hb, n_hb - 1), 0))
    return pl.pallas_call(
        functools.partial(_conv_silu_kernel, seq_len=seq_len, k_scale=k_scale),
        grid=(t // tr, n_slabs),
        in_specs=[pl.BlockSpec((None, tr, tc), cur(0)),
                  pl.BlockSpec((None, _HALO, tc), prev(0)),
                  pl.BlockSpec((None, _HALO, tc), nxt(0)),
                  pl.BlockSpec((None, tr, tc), cur(n_slabs)),
                  pl.BlockSpec((None, _HALO, tc), prev(n_slabs)),
                  pl.BlockSpec((None, _HALO, tc), nxt(n_slabs)),
                  pl.BlockSpec((3, tc), lambda i, j: (0, j)),
                  pl.BlockSpec((3, tc), lambda i, j: (0, j + n_slabs))],
        out_specs=[pl.BlockSpec((None, tr, tc), lambda i, j: (j, i, 0)),
                   pl.BlockSpec((tr // CHUNK, tc, CHUNK), lambda i, j: (i, j, 0))],
        out_shape=[jax.ShapeDtypeStruct((n_slabs, t, tc), BF16),
                   jax.ShapeDtypeStruct((t // CHUNK, n_slabs * tc, CHUNK), BF16)],
        compiler_params=_cparams(2, vmem),
        name=name,
    )(z3, z3, z3, z3, z3, z3, conv_w, conv_w)


def _chunk_scan(x, lane_in_chunk, *, suffix):
    n = x.shape[-1]
    k = 1
    while k < CHUNK:
        if suffix:
            shifted = pltpu.roll(x, n - k, axis=1)
            x = x + jnp.where(lane_in_chunk < CHUNK - k, shifted, 0.0)
        else:
            shifted = pltpu.roll(x, k, axis=1)
            x = x + jnp.where(lane_in_chunk >= k, shifted, 0.0)
        k *= 2
    return x


def _gate_prep_kernel(g_ref, bias_ref, o_ref):
    i_f = g_ref[0] + bias_ref[0]
    lf_f = jax.nn.log_sigmoid(g_ref[1] + bias_ref[1])
    i_b = g_ref[2] + bias_ref[2]
    lf_b = jax.nn.log_sigmoid(g_ref[3] + bias_ref[3])
    lane = lax.rem(lax.broadcasted_iota(jnp.int32, i_f.shape, 1), CHUNK)
    b_f = _chunk_scan(lf_f, lane, suffix=False)
    g_f = b_f + _chunk_scan(lf_f, lane, suffix=True) - lf_f
    b_b = _chunk_scan(lf_b, lane, suffix=True)
    g_b = b_b + _chunk_scan(lf_b, lane, suffix=False) - lf_b
    r_f = i_f - b_f
    r_b = i_b - b_b
    o_ref[0] = b_f
    o_ref[1] = r_f
    o_ref[2] = g_f + r_f
    o_ref[3] = g_f
    o_ref[4] = b_b
    o_ref[5] = r_b
    o_ref[6] = g_b + r_b
    o_ref[7] = g_b


def gate_prep(g4, bias, *, tl, name):
    _, h, t = g4.shape
    tl = min(tl, t)
    assert t % tl == 0 and tl % CHUNK == 0
    vmem = 2 * (_nbytes((4, h, tl), F32) + _nbytes((8, h, tl), F32)) + 16 * _nbytes((h, tl), F32)
    return pl.pallas_call(
        _gate_prep_kernel,
        grid=(t // tl,),
        in_specs=[pl.BlockSpec((4, h, tl), lambda i: (0, 0, i)),
                  pl.BlockSpec((4, h, 1), lambda i: (0, 0, 0))],
        out_specs=pl.BlockSpec((8, h, tl), lambda i: (0, 0, i)),
        out_shape=jax.ShapeDtypeStruct((8, h, t), F32),
        compiler_params=_cparams(1, vmem),
        name=name,
    )(g4, bias)


def _lanes(x, n):
    return x if n == V7X_LANES else jnp.concatenate([x] * (n // V7X_LANES), axis=1)


def _mlstm_chunk(q, kt, v_ext, b_row, r_row, a_row, st_ref, m11, mask, eye):
    dv = v_ext.shape[1] - V7X_LANES
    b_col = jnp.sum(jnp.where(eye, b_row, 0.0), axis=1, keepdims=True)
    b_wide = jnp.broadcast_to(b_col, (b_col.shape[0], V7X_LANES))
    dmat = jnp.where(mask, b_wide + r_row, -jnp.inf)
    inter = b_wide + m11
    m_t = jnp.maximum(inter, jnp.max(dmat, axis=1, keepdims=True))
    w_intra = jnp.exp(dmat - m_t)
    w_inter = jnp.exp(inter - m_t)
    s_qk = jnp.dot(q, kt, preferred_element_type=F32) * w_intra
    intra = jnp.dot(s_qk.astype(BF16), v_ext, preferred_element_type=F32)
    state = st_ref[...]
    carried = jnp.dot(q, state.astype(BF16), preferred_element_type=F32)
    num = intra[:, :dv] + _lanes(w_inter, dv) * carried[:, :dv]
    den = intra[:, dv:] + w_inter * carried[:, dv:]
    h = num / _lanes(jnp.maximum(jnp.abs(den), jnp.exp(-m_t)), dv)
    g11 = a_row[:, 0:1] - r_row[:, 0:1]
    m_new = jnp.maximum(g11 + m11, jnp.max(a_row, axis=1, keepdims=True))
    wa = jnp.exp(a_row - m_new)
    decay = jnp.exp(g11 + m11 - m_new)
    ktw = (kt.astype(F32) * wa).astype(BF16)
    st_ref[...] = decay * state + jnp.dot(ktw, v_ext, preferred_element_type=F32)
    return h, m_new


def _mlstm_kernel(q_ref, kt_ref, v_ref, o_ref, rows_ref, hg_ref, out_ref, acc_ref, stf_ref, stb_ref):
    nc = rows_ref.shape[0]
    L = CHUNK
    stf_ref[...] = jnp.zeros_like(stf_ref)
    stb_ref[...] = jnp.zeros_like(stb_ref)
    t_idx = lax.broadcasted_iota(jnp.int32, (L, L), 0)
    s_idx = lax.broadcasted_iota(jnp.int32, (L, L), 1)
    causal = s_idx <= t_idx
    anti = s_idx >= t_idx
    eye = s_idx == t_idx
    ones = jnp.ones((L, V7X_LANES), BF16)

    def directions(c, m_f, m_b):
        cb = nc - 1 - c
        sf = pl.ds(pl.multiple_of(c * L, L), L)
        sb = pl.ds(pl.multiple_of(cb * L, L), L)
        rows_f = rows_ref[c]
        rows_b = rows_ref[cb]
        h_f, m_f = _mlstm_chunk(q_ref[sf, :], kt_ref[c], jnp.concatenate([v_ref[sf, :], ones], axis=1),
                                rows_f[0:1, :], rows_f[1:2, :], rows_f[2:3, :], stf_ref, m_f, causal, eye)
        h_b, m_b = _mlstm_chunk(q_ref[sb, :], kt_ref[cb], jnp.concatenate([v_ref[sb, :], ones], axis=1),
                                rows_b[4:5, :], rows_b[5:6, :], rows_b[6:7, :], stb_ref, m_b, anti, eye)
        return sf, sb, h_f, h_b, m_f, m_b

    def first_half(c, carry):
        sf, sb, h_f, h_b, m_f, m_b = directions(c, *carry)
        acc_ref[sf, :] = h_f
        acc_ref[sb, :] = h_b
        return m_f, m_b

    def finish(sl, h_new):
        hh = (acc_ref[sl, :] + h_new) * jax.nn.sigmoid(o_ref[sl, :].astype(F32))
        mu = jnp.mean(hh, axis=1, keepdims=True)
        cen = hh - mu
        var = jnp.mean(cen * cen, axis=1, keepdims=True)
        out_ref[sl, :] = (cen * lax.rsqrt(var + LN_EPS) * hg_ref[...]).astype(out_ref.dtype)

    def second_half(c, carry):
        sf, sb, h_f, h_b, m_f, m_b = directions(c, *carry)
        finish(sf, h_f)
        finish(sb, h_b)
        return m_f, m_b

    zero = jnp.zeros((1, 1), F32)
    carry = lax.fori_loop(0, nc // 2, first_half, (zero, zero))
    lax.fori_loop(nc // 2, nc, second_half, carry)


def mlstm(q3, kt, z3, rows, head_g, *, batch, seq_len, name):
    t = q3.shape[1]
    nh, dh = MLSTM_HEADS, MLSTM_HEAD_DIM
    nc = seq_len // CHUNK
    assert nc % 2 == 0
    vmem = 2 * 4 * _nbytes((seq_len, dh), BF16)
    vmem += 2 * _nbytes((seq_len, dh), BF16) + _nbytes((seq_len, dh), F32) + 2 * _nbytes((nc, 8, CHUNK), F32)
    vmem += 4 * _nbytes((dh, dh + V7X_LANES), F32)
    return pl.pallas_call(
        _mlstm_kernel,
        grid=(batch, nh),
        in_specs=[pl.BlockSpec((None, seq_len, dh), lambda b, h: (h, b, 0)),
                  pl.BlockSpec((nc, dh, CHUNK), lambda b, h: (b, h, 0)),
                  pl.BlockSpec((None, seq_len, dh), lambda b, h: (2 * nh + h, b, 0)),
                  pl.BlockSpec((None, seq_len, dh), lambda b, h: (3 * nh + h, b, 0)),
                  pl.BlockSpec((None, nc, 8, CHUNK), lambda b, h: (h, b, 0, 0)),
                  pl.BlockSpec((None, 1, dh), lambda b, h: (h, 0, 0))],
        out_specs=pl.BlockSpec((seq_len, dh), lambda b, h: (b, h)),
        out_shape=jax.ShapeDtypeStruct((t, nh * dh), BF16),
        scratch_shapes=[pltpu.VMEM((seq_len, dh), F32),
                        pltpu.VMEM((dh, dh + V7X_LANES), F32), pltpu.VMEM((dh, dh + V7X_LANES), F32)],
        compiler_params=_cparams(2, vmem),
        name=name,
    )(q3, kt, z3, z3, rows, head_g.reshape(nh, 1, dh))


def _gelu_tanh(x):
    return 0.5 * x * (1.0 + jnp.tanh(math.sqrt(2.0 / math.pi) * (x + 0.044715 * (x * x * x))))


def _sgu_kernel(u_ref, v_ref, ng_ref, nb_ref, ws_ref, bs_ref, o_ref):
    n_chunks = u_ref.shape[0] // CHUNK
    ws = ws_ref[...]
    for c in range(n_chunks):
        sl = slice(c * CHUNK, (c + 1) * CHUNK)
        v = _gelu_tanh(v_ref[sl, :].astype(F32))
        mu = jnp.mean(v, axis=1, keepdims=True)
        cen = v - mu
        var = jnp.mean(cen * cen, axis=1, keepdims=True)
        vn = cen * lax.rsqrt(var + LN_EPS) * ng_ref[...] + nb_ref[...]
        sp = jnp.dot(ws, vn.astype(BF16), preferred_element_type=F32) + bs_ref[...]
        o_ref[sl, :] = (_gelu_tanh(u_ref[sl, :].astype(F32)) * sp).astype(o_ref.dtype)


def spatial_gating(z, norm_g, norm_b, w_s, b_s, *, u_col, v_col, tr, name):
    t = z.shape[0]
    g, dg = SGU_GROUPS, SGU_GROUP_DIM
    tr = min(tr, t)
    assert t % tr == 0 and tr % CHUNK == 0
    vmem = 2 * 3 * _nbytes((tr, dg), BF16) + 8 * _nbytes((CHUNK, dg), F32) * (tr // CHUNK)
    return pl.pallas_call(
        _sgu_kernel,
        grid=(g, t // tr),
        in_specs=[pl.BlockSpec((tr, dg), lambda j, i: (i, u_col + j)),
                  pl.BlockSpec((tr, dg), lambda j, i: (i, v_col + j)),
                  pl.BlockSpec((None, 1, dg), lambda j, i: (j, 0, 0)),
                  pl.BlockSpec((None, 1, dg), lambda j, i: (j, 0, 0)),
                  pl.BlockSpec((None, CHUNK, CHUNK), lambda j, i: (j, 0, 0)),
                  pl.BlockSpec((None, CHUNK, 1), lambda j, i: (j, 0, 0))],
        out_specs=pl.BlockSpec((tr, dg), lambda j, i: (i, j)),
        out_shape=jax.ShapeDtypeStruct((t, g * dg), BF16),
        compiler_params=_cparams(2, vmem),
        name=name,
    )(z, z, norm_g.reshape(g, 1, dg), norm_b.reshape(g, 1, dg), w_s.astype(BF16), b_s.reshape(g, CHUNK, 1))


def _swiglu(x, wg, wu):
    gate = jnp.dot(x, wg, preferred_element_type=F32)
    up = jnp.dot(x, wu, preferred_element_type=F32)
    return gate * jax.nn.sigmoid(gate) * up


def _routed_up_kernel(x_ref, comb_ref, wg_ref, wu_ref, h_ref, wgc_ref, wuc_ref):
    per = wg_ref.shape[0]
    fe = wg_ref.shape[2]
    j = pl.program_id(0)

    @pl.when(pl.program_id(1) == 0)
    def _():
        for e in range(per):
            wgc_ref[:, e * fe:(e + 1) * fe] = wg_ref[e]
            wuc_ref[:, e * fe:(e + 1) * fe] = wu_ref[e]

    h = _swiglu(x_ref[...], wgc_ref[...], wuc_ref[...])
    e_row = lax.broadcasted_iota(jnp.int32, (V7X_LANES, per * fe), 0)
    e_col = lax.broadcasted_iota(jnp.int32, (V7X_LANES, per * fe), 1) // fe + j * per
    expand = jnp.where(e_row == e_col, 1.0, 0.0).astype(BF16)
    comb = comb_ref[...]
    c_hi = comb.astype(BF16)
    c_lo = (comb - c_hi.astype(F32)).astype(BF16)
    scale = jnp.dot(jnp.concatenate([c_hi, c_lo], axis=1), jnp.concatenate([expand, expand], axis=0),
                    preferred_element_type=F32)
    h_ref[...] = (h * scale).astype(h_ref.dtype)


def routed_up(xb, comb, wg, wu, *, layer, tm, per, name):
    t, d = xb.shape
    _, n_e, _, fe = wg.shape
    tm = min(tm, t)
    assert t % tm == 0 and n_e % per == 0
    tf = per * fe
    vmem = 2 * (_nbytes((tm, d), BF16) + _nbytes((tm, V7X_LANES), F32) + 2 * _nbytes((d, tf), BF16))
    vmem += 2 * _nbytes((tm, tf), BF16) + 2 * _nbytes((d, tf), BF16) + 5 * _nbytes((tm, tf), F32)
    return pl.pallas_call(
        _routed_up_kernel,
        grid=(n_e // per, t // tm),
        in_specs=[pl.BlockSpec((tm, d), lambda j, i: (i, 0)),
                  pl.BlockSpec((tm, V7X_LANES), lambda j, i: (i, 0)),
                  pl.BlockSpec((None, per, d, fe), lambda j, i: (layer, j, 0, 0)),
                  pl.BlockSpec((None, per, d, fe), lambda j, i: (layer, j, 0, 0))],
        out_specs=pl.BlockSpec((tm, tf), lambda j, i: (i, j)),
        out_shape=jax.ShapeDtypeStruct((t, n_e * fe), BF16),
        scratch_shapes=[pltpu.VMEM((d, tf), BF16), pltpu.VMEM((d, tf), BF16)],
        compiler_params=_cparams(2, vmem),
        name=name,
    )(xb, comb, wg, wu)


def _shared_up_kernel(x_ref, wg_ref, wu_ref, h_ref):
    h_ref[...] = _swiglu(x_ref[...], wg_ref[...], wu_ref[...]).astype(h_ref.dtype)


def shared_up(xb, wg, wu, *, layer, tm, tf, name):
    t, d = xb.shape
    f = wg.shape[2]
    tm, tf = min(tm, t), min(tf, f)
    assert t % tm == 0 and f % tf == 0
    vmem = 2 * (_nbytes((tm, d), BF16) + 2 * _nbytes((d, tf), BF16) + _nbytes((tm, tf), BF16))
    vmem += 4 * _nbytes((tm, tf), F32)
    return pl.pallas_call(
        _shared_up_kernel,
        grid=(f // tf, t // tm),
        in_specs=[pl.BlockSpec((tm, d), lambda j, i: (i, 0)),
                  pl.BlockSpec((None, d, tf), lambda j, i: (layer, 0, j)),
                  pl.BlockSpec((None, d, tf), lambda j, i: (layer, 0, j))],
        out_specs=pl.BlockSpec((tm, tf), lambda j, i: (i, j)),
        out_shape=jax.ShapeDtypeStruct((t, f), BF16),
        compiler_params=_cparams(2, vmem),
        name=name,
    )(xb, wg, wu)


def kernel(x, w_in, b_gates, conv_qk, mlstm_norm_g, sgu_norm_g, sgu_norm_b, w_spatial, b_spatial, w_out,
           ln1_g, ln1_b, w_router, b_router, w_exp_gate, w_exp_up, w_exp_down, w_sh_gate, w_sh_up, w_sh_down,
           ln2_g, ln2_b):
    bsz, seq_len, d = x.shape
    depth = w_in.shape[0]
    t = bsz * seq_len
    nh, dh = MLSTM_HEADS, MLSTM_HEAD_DIM
    w_m = nh * dh
    w_s = SGU_GROUPS * SGU_GROUP_DIM
    off_g = 4 * w_m
    n_e, f_e = w_exp_gate.shape[1], w_exp_gate.shape[3]
    nc_total = t // CHUNK

    xf = x.reshape(t, d)
    xb = xf.astype(BF16)
    lane_pad = V7X_LANES - N_EXPERTS

    w_in_t = jnp.swapaxes(w_in, 1, 2)
    w_usv_t = w_in_t[:, off_g + N_GATES:, :].astype(BF16)
    w_gate_t = jnp.pad(w_in_t[:, off_g:off_g + N_GATES, :], ((0, 0), (0, V7X_LANES - N_GATES), (0, 0))).astype(BF16)
    w_r = jnp.pad(w_router, ((0, 0), (0, 0), (0, lane_pad))).astype(BF16)
    b_r = jnp.pad(b_router, ((0, 0), (0, lane_pad))).reshape(depth, 1, V7X_LANES)
    wg_e, wu_e = w_exp_gate.astype(BF16), w_exp_up.astype(BF16)
    wd_e = w_exp_down.reshape(depth, n_e * f_e, d).astype(BF16)
    wg_s, wu_s, wd_s = w_sh_gate.astype(BF16), w_sh_up.astype(BF16), w_sh_down.astype(BF16)

    for l in range(depth):
        z_m = matmul(xb, w_in_t, tm=1024, tn=1024, out_dtype=BF16, n_cols=off_g, layer=l, rhs_t=True, slab=dh,
                     name=f"in_proj_m_{l}")
        z_s = matmul(xb, w_usv_t, tm=1024, tn=1024, out_dtype=BF16, layer=l, rhs_t=True, name=f"in_proj_s_{l}")
        g_raw = matmul(xb, w_gate_t, tm=1024, tn=V7X_LANES, out_dtype=F32, layer=l, rhs_t=True,
                       name=f"gate_proj_{l}")
        g4 = g_raw[:, :N_GATES].T.reshape(4, nh, t)
        tables = gate_prep(g4, b_gates[l].reshape(4, nh, 1), tl=2048, name=f"gate_prep_{l}")
        rows = tables.transpose(1, 0, 2).reshape(nh, 8, nc_total, CHUNK).transpose(0, 2, 1, 3)
        q_c, kt_c = conv_silu(z_m, conv_qk[l], n_slabs=nh, seq_len=seq_len, k_scale=dh ** -0.5, tr=1024,
                              name=f"conv_silu_{l}")
        h_m = mlstm(q_c, kt_c, z_m, rows, mlstm_norm_g[l], batch=bsz, seq_len=seq_len, name=f"mlstm_{l}")
        h_s = spatial_gating(z_s, sgu_norm_g[l], sgu_norm_b[l], w_spatial[l], b_spatial[l],
                             u_col=0, v_col=w_s // SGU_GROUP_DIM, tr=1024, name=f"sgu_{l}")
        mix = matmul_cat(h_m, h_s, w_out, w_out, tm=1024, tn=1024, out_dtype=BF16, layer=l, row_blocks=(0, 1),
                         name=f"out_proj_{l}")
        x1, x1b, comb = add_layer_norm(xf, mix, ln1_g[l], ln1_b[l], tm=256, router=(w_r[l], b_r[l]),
                                       name=f"ln1_{l}")
        hid_r = routed_up(x1b, comb, wg_e, wu_e, layer=l, tm=1024, per=4, name=f"routed_up_{l}")
        hid_s = shared_up(x1b, wg_s, wu_s, layer=l, tm=1024, tf=512, name=f"shared_up_{l}")
        y = matmul_cat(hid_r, hid_s, wd_e, wd_s, tm=512, tn=1024, out_dtype=BF16, layer=l, single=True,
                       name=f"ffn_down_{l}")
        xf, xb, _ = add_layer_norm(x1, y, ln2_g[l], ln2_b[l], tm=256, bf16_copy=l + 1 < depth, name=f"ln2_{l}")

    return xf.reshape(bsz, seq_len, d)
```

```python
import functools
import math

import jax
import jax.numpy as jnp
from jax import lax
from jax.experimental import pallas as pl
from jax.experimental.pallas import tpu as pltpu

MLSTM_HEADS = 8
MLSTM_HEAD_DIM = 256
CHUNK = 256
SGU_CHUNK = 128
SGU_GROUPS = 8
SGU_GROUP_DIM = 256
N_GATES = 4 * MLSTM_HEADS
N_EXPERTS = 64
TOP_K = 8
D_EXPERT = 128
ROUTE_SCALE = 2.5
DEPTH_FOR_NORM = 4
DEEPNORM_ALPHA = (2 * DEPTH_FOR_NORM) ** 0.25
LN_EPS = 1e-5

V7X_LANES = 128
V7X_VMEM_LIMIT_BYTES = 60000 * 1024

F32 = jnp.float32
BF16 = jnp.bfloat16


def _cparams(n_axes, vmem_bytes):
    limit = int(min(max(vmem_bytes * 5 // 4 + (4 << 20), 16 << 20), V7X_VMEM_LIMIT_BYTES))
    return pltpu.CompilerParams(dimension_semantics=("arbitrary",) * n_axes, vmem_limit_bytes=limit)


def _nbytes(shape, dtype):
    return math.prod(shape) * jnp.dtype(dtype).itemsize


def _weight_spec(w, rows, tn, layer, row_block=0, transposed=False, single=False):
    mode = dict(pipeline_mode=pl.Buffered(1)) if (single or w.dtype != BF16) else {}
    shape = (tn, rows) if transposed else (rows, tn)
    index = (lambda j: (j, row_block)) if transposed else (lambda j: (row_block, j))
    if w.ndim == 2:
        return pl.BlockSpec(shape, lambda j, i: index(j), **mode)
    return pl.BlockSpec((None,) + shape, lambda j, i: (layer,) + index(j), **mode)


_NT = (((1,), (1,)), ((), ()))


def _mm_kernel(a_ref, b_ref, o_ref, *scratch, rhs_t):
    if scratch:
        (wb_ref,) = scratch

        @pl.when(pl.program_id(1) == 0)
        def _():
            wb_ref[...] = b_ref[...].astype(BF16)

        b = wb_ref[...]
    else:
        b = b_ref[...]
    if rhs_t:
        acc = lax.dot_general(a_ref[...], b, _NT, preferred_element_type=F32)
    else:
        acc = jnp.dot(a_ref[...], b, preferred_element_type=F32)
    if len(o_ref.shape) == 2:
        o_ref[...] = acc.astype(o_ref.dtype)
    else:
        w = o_ref.shape[2]
        for c in range(o_ref.shape[0]):
            o_ref[c] = acc[:, c * w:(c + 1) * w].astype(o_ref.dtype)


def matmul(a, b, *, tm, tn, out_dtype, name, n_cols=None, layer=None, rhs_t=False, slab=None):
    m, k = a.shape
    n = (b.shape[-2] if rhs_t else b.shape[-1]) if n_cols is None else n_cols
    tm, tn = min(tm, m), min(tn, n)
    assert m % tm == 0 and n % tn == 0
    cast = b.dtype != BF16
    vmem = 2 * (_nbytes((tm, k), a.dtype) + _nbytes((tm, tn), out_dtype)) + _nbytes((tm, tn), F32)
    vmem += _nbytes((k, tn), F32) + _nbytes((k, tn), BF16) if cast else 2 * _nbytes((k, tn), BF16)
    if slab is None:
        out_spec = pl.BlockSpec((tm, tn), lambda j, i: (i, j))
        out_shape = jax.ShapeDtypeStruct((m, n), out_dtype)
    else:
        assert tn % slab == 0
        out_spec = pl.BlockSpec((tn // slab, tm, slab), lambda j, i: (j, i, 0))
        out_shape = jax.ShapeDtypeStruct((n // slab, m, slab), out_dtype)
    return pl.pallas_call(
        functools.partial(_mm_kernel, rhs_t=rhs_t),
        grid=(n // tn, m // tm),
        in_specs=[pl.BlockSpec((tm, k), lambda j, i: (i, 0)), _weight_spec(b, k, tn, layer, transposed=rhs_t)],
        out_specs=out_spec,
        out_shape=out_shape,
        scratch_shapes=[pltpu.VMEM((tn, k) if rhs_t else (k, tn), BF16)] if cast else [],
        compiler_params=_cparams(2, vmem),
        name=name,
    )(a, b)


def _mm2_kernel(a1_ref, a2_ref, b1_ref, b2_ref, o_ref, *scratch):
    if scratch:
        w1_ref, w2_ref = scratch

        @pl.when(pl.program_id(1) == 0)
        def _():
            w1_ref[...] = b1_ref[...].astype(BF16)
            w2_ref[...] = b2_ref[...].astype(BF16)

        b1, b2 = w1_ref[...], w2_ref[...]
    else:
        b1, b2 = b1_ref[...], b2_ref[...]
    acc = jnp.dot(a1_ref[...], b1, preferred_element_type=F32)
    acc = acc + jnp.dot(a2_ref[...], b2, preferred_element_type=F32)
    o_ref[...] = acc.astype(o_ref.dtype)


def matmul_cat(a1, a2, b1, b2, *, tm, tn, out_dtype, name, layer=None, row_blocks=(0, 0), single=False):
    m, k1 = a1.shape
    _, k2 = a2.shape
    n = b1.shape[-1]
    tm, tn = min(tm, m), min(tn, n)
    assert m % tm == 0 and n % tn == 0
    k = k1 + k2
    assert row_blocks[0] == row_blocks[1] or k1 == k2
    cast = b1.dtype != BF16
    vmem = 2 * (_nbytes((tm, k), a1.dtype) + _nbytes((tm, tn), out_dtype)) + _nbytes((tm, tn), F32)
    vmem += _nbytes((k, tn), F32) + _nbytes((k, tn), BF16) if cast else (1 if single else 2) * _nbytes((k, tn), BF16)
    return pl.pallas_call(
        _mm2_kernel,
        grid=(n // tn, m // tm),
        in_specs=[pl.BlockSpec((tm, k1), lambda j, i: (i, 0)),
                  pl.BlockSpec((tm, k2), lambda j, i: (i, 0)),
                  _weight_spec(b1, k1, tn, layer, row_blocks[0], single=single),
                  _weight_spec(b2, k2, tn, layer, row_blocks[1], single=single)],
        out_specs=pl.BlockSpec((tm, tn), lambda j, i: (i, j)),
        out_shape=jax.ShapeDtypeStruct((m, n), out_dtype),
        scratch_shapes=[pltpu.VMEM((k1, tn), BF16), pltpu.VMEM((k2, tn), BF16)] if cast else [],
        compiler_params=_cparams(2, vmem),
        name=name,
    )(a1, a2, b1, b2)


def _top_k_gate(logits, bias):
    lane = lax.broadcasted_iota(jnp.int32, logits.shape, 1)
    scores = jax.nn.sigmoid(logits)
    sel = jnp.where(lane < N_EXPERTS, scores + bias, -jnp.inf)
    picked = jnp.zeros(logits.shape, jnp.bool_)
    for _ in range(TOP_K):
        best = jnp.max(sel, axis=1, keepdims=True)
        first = jnp.min(jnp.where(sel == best, lane, V7X_LANES), axis=1, keepdims=True)
        hit = lane == first
        picked = jnp.logical_or(picked, hit)
        sel = jnp.where(hit, -jnp.inf, sel)
    w = jnp.where(picked, scores, 0.0)
    return w / jnp.sum(w, axis=1, keepdims=True) * ROUTE_SCALE


def _add_ln_kernel(x_ref, r_ref, g_ref, b_ref, *rest, bf16_copy, route):
    if route:
        wr_ref, br_ref, *outs = rest
    else:
        outs = rest
    v = DEEPNORM_ALPHA * x_ref[...] + r_ref[...].astype(F32)
    mu = jnp.mean(v, axis=-1, keepdims=True)
    c = v - mu
    var = jnp.mean(c * c, axis=-1, keepdims=True)
    y = c * lax.rsqrt(var + LN_EPS) * g_ref[...] + b_ref[...]
    outs[0][...] = y
    if bf16_copy:
        yb = y.astype(BF16)
        outs[1][...] = yb
        if route:
            logits = jnp.dot(yb, wr_ref[...], preferred_element_type=F32)
            outs[2][...] = _top_k_gate(logits, br_ref[...])


def add_layer_norm(x, r, g, b, *, tm, name, bf16_copy=True, router=None):
    t, d = x.shape
    tm = min(tm, t)
    assert t % tm == 0 and (router is None or bf16_copy)
    vmem = 2 * (3 * _nbytes((tm, d), F32) + _nbytes((tm, d), BF16)) + 4 * _nbytes((tm, d), F32)
    row = lambda i: (i, 0)
    fixed = lambda i: (0, 0)
    in_specs = [pl.BlockSpec((tm, d), row), pl.BlockSpec((tm, d), row),
                pl.BlockSpec((1, d), fixed), pl.BlockSpec((1, d), fixed)]
    operands = [x, r, g.reshape(1, d), b.reshape(1, d)]
    out_specs = [pl.BlockSpec((tm, d), row)]
    out_shape = [jax.ShapeDtypeStruct((t, d), F32)]
    if bf16_copy:
        out_specs.append(pl.BlockSpec((tm, d), row))
        out_shape.append(jax.ShapeDtypeStruct((t, d), BF16))
    if router is not None:
        in_specs += [pl.BlockSpec((d, V7X_LANES), fixed), pl.BlockSpec((1, V7X_LANES), fixed)]
        operands += list(router)
        out_specs.append(pl.BlockSpec((tm, V7X_LANES), row))
        out_shape.append(jax.ShapeDtypeStruct((t, V7X_LANES), F32))
        vmem += 2 * _nbytes((d, V7X_LANES), BF16) + 10 * _nbytes((tm, V7X_LANES), F32)
    out = pl.pallas_call(
        functools.partial(_add_ln_kernel, bf16_copy=bf16_copy, route=router is not None),
        grid=(t // tm,),
        in_specs=in_specs,
        out_specs=out_specs,
        out_shape=out_shape,
        compiler_params=_cparams(1, vmem),
        name=name,
    )(*operands)
    return out[0], (out[1] if bf16_copy else None), (out[2] if router is not None else None)


_HALO = 16


def _conv3_silu(z_ref, zp_ref, zn_ref, w_ref, seq_len):
    tr = z_ref.shape[0]
    x = z_ref[...].astype(F32)
    row = lax.broadcasted_iota(jnp.int32, x.shape, 0)
    t0 = pl.program_id(0) * tr
    prev_row = jnp.where(lax.rem(t0, seq_len) == 0, 0.0, zp_ref[_HALO - 1:_HALO, :].astype(F32))
    next_row = jnp.where(lax.rem(t0 + tr, seq_len) == 0, 0.0, zn_ref[0:1, :].astype(F32))
    x_prev = jnp.where(row == 0, prev_row, pltpu.roll(x, 1, axis=0))
    x_next = jnp.where(row == tr - 1, next_row, pltpu.roll(x, tr - 1, axis=0))
    y = w_ref[0:1, :] * x_prev + w_ref[1:2, :] * x + w_ref[2:3, :] * x_next
    return y * jax.nn.sigmoid(y)


def _conv_silu_kernel(q_ref, qp_ref, qn_ref, k_ref, kp_ref, kn_ref, wq_ref, wk_ref, oq_ref, okt_ref,
                      *, seq_len, k_scale):
    oq_ref[...] = _conv3_silu(q_ref, qp_ref, qn_ref, wq_ref, seq_len).astype(oq_ref.dtype)
    k = _conv3_silu(k_ref, kp_ref, kn_ref, wk_ref, seq_len) * k_scale
    for c in range(okt_ref.shape[0]):
        okt_ref[c] = k[c * CHUNK:(c + 1) * CHUNK, :].T.astype(okt_ref.dtype)


def conv_silu(z3, conv_w, *, n_slabs, seq_len, k_scale, tr, name):
    _, t, tc = z3.shape
    tr = min(tr, seq_len)
    assert seq_len % tr == 0 and tr % CHUNK == 0
    hb = tr // _HALO
    n_hb = t // _HALO
    vmem = 2 * (3 * _nbytes((tr, tc), BF16) + 4 * _nbytes((_HALO, tc), BF16) + _nbytes((tr, tc), BF16))
    vmem += 12 * _nbytes((tr, tc), F32)
    cur = lambda off: (lambda i, j: (j + off, i, 0))
    prev = lambda off: (lambda i, j: (j + off, jnp.maximum(i * hb - 1, 0), 0))
    nxt = lambda off: (lambda i, j: (j + off, jnp.minimum((i + 1) * hb, n_hb - 1), 0))
    return pl.pallas_call(
        functools.partial(_conv_silu_kernel, seq_len=seq_len, k_scale=k_scale),
        grid=(t // tr, n_slabs),
        in_specs=[pl.BlockSpec((None, tr, tc), cur(0)),
                  pl.BlockSpec((None, _HALO, tc), prev(0)),
                  pl.BlockSpec((None, _HALO, tc), nxt(0)),
                  pl.BlockSpec((None, tr, tc), cur(n_slabs)),
                  pl.BlockSpec((None, _HALO, tc), prev(n_slabs)),
                  pl.BlockSpec((None, _HALO, tc), nxt(n_slabs)),
                  pl.BlockSpec((3, tc), lambda i, j: (0, j)),
                  pl.BlockSpec((3, tc), lambda i, j: (0, j + n_slabs))],
        out_specs=[pl.BlockSpec((None, tr, tc), lambda i, j: (j, i, 0)),
                   pl.BlockSpec((tr // CHUNK, tc, CHUNK), lambda i, j: (i, j, 0))],
        out_shape=[jax.ShapeDtypeStruct((n_slabs, t, tc), BF16),
                   jax.ShapeDtypeStruct((t // CHUNK, n_slabs * tc, CHUNK), BF16)],
        compiler_params=_cparams(2, vmem),
        name=name,
    )(z3, z3, z3, z3, z3, z3, conv_w, conv_w)


def _chunk_scan(x, lane_in_chunk, *, suffix):
    n = x.shape[-1]
    k = 1
    while k < CHUNK:
        if suffix:
            shifted = pltpu.roll(x, n - k, axis=1)
            x = x + jnp.where(lane_in_chunk < CHUNK - k, shifted, 0.0)
        else:
            shifted = pltpu.roll(x, k, axis=1)
            x = x + jnp.where(lane_in_chunk >= k, shifted, 0.0)
        k *= 2
    return x


def _gate_prep_kernel(g_ref, bias_ref, o_ref):
    i_f = g_ref[0] + bias_ref[0]
    lf_f = jax.nn.log_sigmoid(g_ref[1] + bias_ref[1])
    i_b = g_ref[2] + bias_ref[2]
    lf_b = jax.nn.log_sigmoid(g_ref[3] + bias_ref[3])
    lane = lax.rem(lax.broadcasted_iota(jnp.int32, i_f.shape, 1), CHUNK)
    b_f = _chunk_scan(lf_f, lane, suffix=False)
    g_f = b_f + _chunk_scan(lf_f, lane, suffix=True) - lf_f
    b_b = _chunk_scan(lf_b, lane, suffix=True)
    g_b = b_b + _chunk_scan(lf_b, lane, suffix=False) - lf_b
    r_f = i_f - b_f
    r_b = i_b - b_b
    o_ref[0] = b_f
    o_ref[1] = r_f
    o_ref[2] = g_f + r_f
    o_ref[3] = g_f
    o_ref[4] = b_b
    o_ref[5] = r_b
    o_ref[6] = g_b + r_b
    o_ref[7] = g_b


def gate_prep(g4, bias, *, tl, name):
    _, h, t = g4.shape
    tl = min(tl, t)
    assert t % tl == 0 and tl % CHUNK == 0
    vmem = 2 * (_nbytes((4, h, tl), F32) + _nbytes((8, h, tl), F32)) + 16 * _nbytes((h, tl), F32)
    return pl.pallas_call(
        _gate_prep_kernel,
        grid=(t // tl,),
        in_specs=[pl.BlockSpec((4, h, tl), lambda i: (0, 0, i)),
                  pl.BlockSpec((4, h, 1), lambda i: (0, 0, 0))],
        out_specs=pl.BlockSpec((8, h, tl), lambda i: (0, 0, i)),
        out_shape=jax.ShapeDtypeStruct((8, h, t), F32),
        compiler_params=_cparams(1, vmem),
        name=name,
    )(g4, bias)


def _lanes(x, n):
    return x if n == V7X_LANES else jnp.concatenate([x] * (n // V7X_LANES), axis=1)


def _mlstm_chunk(q, kt, v_ext, b_row, r_row, a_row, st_ref, m11, mask, eye):
    dv = v_ext.shape[1] - V7X_LANES
    b_col = jnp.sum(jnp.where(eye, b_row, 0.0), axis=1, keepdims=True)
    b_wide = jnp.broadcast_to(b_col, (b_col.shape[0], V7X_LANES))
    n_src = r_row.shape[1]
    dmat = jnp.where(mask, _lanes(b_wide, n_src) + r_row, -jnp.inf)
    inter = b_wide + m11
    m_t = jnp.maximum(inter, jnp.max(dmat, axis=1, keepdims=True))
    w_intra = jnp.exp(dmat - _lanes(m_t, n_src))
    w_inter = jnp.exp(inter - m_t)
    s_qk = jnp.dot(q, kt, preferred_element_type=F32) * w_intra
    intra = jnp.dot(s_qk.astype(BF16), v_ext, preferred_element_type=F32)
    state = st_ref[...]
    carried = jnp.dot(q, state.astype(BF16), preferred_element_type=F32)
    num = intra[:, :dv] + _lanes(w_inter, dv) * carried[:, :dv]
    den = intra[:, dv:] + w_inter * carried[:, dv:]
    h = num / _lanes(jnp.maximum(jnp.abs(den), jnp.exp(-m_t)), dv)
    g11 = a_row[:, 0:1] - r_row[:, 0:1]
    m_new = jnp.maximum(g11 + m11, jnp.max(a_row, axis=1, keepdims=True))
    wa = jnp.exp(a_row - m_new)
    decay = jnp.exp(g11 + m11 - m_new)
    ktw = (kt.astype(F32) * wa).astype(BF16)
    st_ref[...] = decay * state + jnp.dot(ktw, v_ext, preferred_element_type=F32)
    return h, m_new


def _mlstm_kernel(q_ref, kt_ref, v_ref, o_ref, rows_ref, hg_ref, out_ref, acc_ref, stf_ref, stb_ref):
    nc = rows_ref.shape[0]
    L = CHUNK
    stf_ref[...] = jnp.zeros_like(stf_ref)
    stb_ref[...] = jnp.zeros_like(stb_ref)
    t_idx = lax.broadcasted_iota(jnp.int32, (L, L), 0)
    s_idx = lax.broadcasted_iota(jnp.int32, (L, L), 1)
    causal = s_idx <= t_idx
    anti = s_idx >= t_idx
    eye = s_idx == t_idx
    ones = jnp.ones((L, V7X_LANES), BF16)

    def directions(c, m_f, m_b):
        cb = nc - 1 - c
        sf = pl.ds(pl.multiple_of(c * L, L), L)
        sb = pl.ds(pl.multiple_of(cb * L, L), L)
        rows_f = rows_ref[c]
        rows_b = rows_ref[cb]
        h_f, m_f = _mlstm_chunk(q_ref[sf, :], kt_ref[c], jnp.concatenate([v_ref[sf, :], ones], axis=1),
                                rows_f[0:1, :], rows_f[1:2, :], rows_f[2:3, :], stf_ref, m_f, causal, eye)
        h_b, m_b = _mlstm_chunk(q_ref[sb, :], kt_ref[cb], jnp.concatenate([v_ref[sb, :], ones], axis=1),
                                rows_b[4:5, :], rows_b[5:6, :], rows_b[6:7, :], stb_ref, m_b, anti, eye)
        return sf, sb, h_f, h_b, m_f, m_b

    def first_half(c, carry):
        sf, sb, h_f, h_b, m_f, m_b = directions(c, *carry)
        acc_ref[sf, :] = h_f
        acc_ref[sb, :] = h_b
        return m_f, m_b

    def finish(sl, h_new):
        hh = (acc_ref[sl, :] + h_new) * jax.nn.sigmoid(o_ref[sl, :].astype(F32))
        mu = jnp.mean(hh, axis=1, keepdims=True)
        cen = hh - mu
        var = jnp.mean(cen * cen, axis=1, keepdims=True)
        out_ref[sl, :] = (cen * lax.rsqrt(var + LN_EPS) * hg_ref[...]).astype(out_ref.dtype)

    def second_half(c, carry):
        sf, sb, h_f, h_b, m_f, m_b = directions(c, *carry)
        finish(sf, h_f)
        finish(sb, h_b)
        return m_f, m_b

    zero = jnp.zeros((1, 1), F32)
    carry = lax.fori_loop(0, nc // 2, first_half, (zero, zero))
    lax.fori_loop(nc // 2, nc, second_half, carry)


def mlstm(q3, kt, z3, rows, head_g, *, batch, seq_len, name):
    t = q3.shape[1]
    nh, dh = MLSTM_HEADS, MLSTM_HEAD_DIM
    nc = seq_len // CHUNK
    assert nc % 2 == 0
    vmem = 2 * 4 * _nbytes((seq_len, dh), BF16)
    vmem += 2 * _nbytes((seq_len, dh), BF16) + _nbytes((seq_len, dh), F32) + 2 * _nbytes((nc, 8, CHUNK), F32)
    vmem += 4 * _nbytes((dh, dh + V7X_LANES), F32)
    return pl.pallas_call(
        _mlstm_kernel,
        grid=(batch, nh),
        in_specs=[pl.BlockSpec((None, seq_len, dh), lambda b, h: (h, b, 0)),
                  pl.BlockSpec((nc, dh, CHUNK), lambda b, h: (b, h, 0)),
                  pl.BlockSpec((None, seq_len, dh), lambda b, h: (2 * nh + h, b, 0)),
                  pl.BlockSpec((None, seq_len, dh), lambda b, h: (3 * nh + h, b, 0)),
                  pl.BlockSpec((None, nc, 8, CHUNK), lambda b, h: (h, b, 0, 0)),
                  pl.BlockSpec((None, 1, dh), lambda b, h: (h, 0, 0))],
        out_specs=pl.BlockSpec((seq_len, dh), lambda b, h: (b, h)),
        out_shape=jax.ShapeDtypeStruct((t, nh * dh), BF16),
        scratch_shapes=[pltpu.VMEM((seq_len, dh), F32),
                        pltpu.VMEM((dh, dh + V7X_LANES), F32), pltpu.VMEM((dh, dh + V7X_LANES), F32)],
        compiler_params=_cparams(2, vmem),
        name=name,
    )(q3, kt, z3, z3, rows, head_g.reshape(nh, 1, dh))


def _gelu_tanh(x):
    return 0.5 * x * (1.0 + jnp.tanh(math.sqrt(2.0 / math.pi) * (x + 0.044715 * (x * x * x))))


def _sgu_kernel(u_ref, v_ref, ng_ref, nb_ref, ws_ref, bs_ref, o_ref):
    n_chunks = u_ref.shape[0] // SGU_CHUNK
    ws = ws_ref[...]
    for c in range(n_chunks):
        sl = slice(c * SGU_CHUNK, (c + 1) * SGU_CHUNK)
        v = _gelu_tanh(v_ref[sl, :].astype(F32))
        mu = jnp.mean(v, axis=1, keepdims=True)
        cen = v - mu
        var = jnp.mean(cen * cen, axis=1, keepdims=True)
        vn = cen * lax.rsqrt(var + LN_EPS) * ng_ref[...] + nb_ref[...]
        sp = jnp.dot(ws, vn.astype(BF16), preferred_element_type=F32) + bs_ref[...]
        o_ref[sl, :] = (_gelu_tanh(u_ref[sl, :].astype(F32)) * sp).astype(o_ref.dtype)


def spatial_gating(z, norm_g, norm_b, w_s, b_s, *, u_col, v_col, tr, name):
    t = z.shape[0]
    g, dg = SGU_GROUPS, SGU_GROUP_DIM
    tr = min(tr, t)
    assert t % tr == 0 and tr % SGU_CHUNK == 0
    vmem = 2 * 3 * _nbytes((tr, dg), BF16) + 8 * _nbytes((SGU_CHUNK, dg), F32) * (tr // SGU_CHUNK)
    return pl.pallas_call(
        _sgu_kernel,
        grid=(g, t // tr),
        in_specs=[pl.BlockSpec((tr, dg), lambda j, i: (i, u_col + j)),
                  pl.BlockSpec((tr, dg), lambda j, i: (i, v_col + j)),
                  pl.BlockSpec((None, 1, dg), lambda j, i: (j, 0, 0)),
                  pl.BlockSpec((None, 1, dg), lambda j, i: (j, 0, 0)),
                  pl.BlockSpec((None, SGU_CHUNK, SGU_CHUNK), lambda j, i: (j, 0, 0)),
                  pl.BlockSpec((None, SGU_CHUNK, 1), lambda j, i: (j, 0, 0))],
        out_specs=pl.BlockSpec((tr, dg), lambda j, i: (i, j)),
        out_shape=jax.ShapeDtypeStruct((t, g * dg), BF16),
        compiler_params=_cparams(2, vmem),
        name=name,
    )(z, z, norm_g.reshape(g, 1, dg), norm_b.reshape(g, 1, dg), w_s.astype(BF16), b_s.reshape(g, SGU_CHUNK, 1))


def _swiglu(x, wg, wu):
    gate = jnp.dot(x, wg, preferred_element_type=F32)
    up = jnp.dot(x, wu, preferred_element_type=F32)
    return gate * jax.nn.sigmoid(gate) * up


def _routed_up_kernel(x_ref, comb_ref, wg_ref, wu_ref, h_ref, wgc_ref, wuc_ref):
    per = wg_ref.shape[0]
    fe = wg_ref.shape[2]
    j = pl.program_id(0)

    @pl.when(pl.program_id(1) == 0)
    def _():
        for e in range(per):
            wgc_ref[:, e * fe:(e + 1) * fe] = wg_ref[e]
            wuc_ref[:, e * fe:(e + 1) * fe] = wu_ref[e]

    h = _swiglu(x_ref[...], wgc_ref[...], wuc_ref[...])
    e_row = lax.broadcasted_iota(jnp.int32, (V7X_LANES, per * fe), 0)
    e_col = lax.broadcasted_iota(jnp.int32, (V7X_LANES, per * fe), 1) // fe + j * per
    expand = jnp.where(e_row == e_col, 1.0, 0.0).astype(BF16)
    comb = comb_ref[...]
    c_hi = comb.astype(BF16)
    c_lo = (comb - c_hi.astype(F32)).astype(BF16)
    scale = jnp.dot(jnp.concatenate([c_hi, c_lo], axis=1), jnp.concatenate([expand, expand], axis=0),
                    preferred_element_type=F32)
    h_ref[...] = (h * scale).astype(h_ref.dtype)


def routed_up(xb, comb, wg, wu, *, layer, tm, per, name):
    t, d = xb.shape
    _, n_e, _, fe = wg.shape
    tm = min(tm, t)
    assert t % tm == 0 and n_e % per == 0
    tf = per * fe
    vmem = 2 * (_nbytes((tm, d), BF16) + _nbytes((tm, V7X_LANES), F32) + 2 * _nbytes((d, tf), BF16))
    vmem += 2 * _nbytes((tm, tf), BF16) + 2 * _nbytes((d, tf), BF16) + 5 * _nbytes((tm, tf), F32)
    return pl.pallas_call(
        _routed_up_kernel,
        grid=(n_e // per, t // tm),
        in_specs=[pl.BlockSpec((tm, d), lambda j, i: (i, 0)),
                  pl.BlockSpec((tm, V7X_LANES), lambda j, i: (i, 0)),
                  pl.BlockSpec((None, per, d, fe), lambda j, i: (layer, j, 0, 0)),
                  pl.BlockSpec((None, per, d, fe), lambda j, i: (layer, j, 0, 0))],
        out_specs=pl.BlockSpec((tm, tf), lambda j, i: (i, j)),
        out_shape=jax.ShapeDtypeStruct((t, n_e * fe), BF16),
        scratch_shapes=[pltpu.VMEM((d, tf), BF16), pltpu.VMEM((d, tf), BF16)],
        compiler_params=_cparams(2, vmem),
        name=name,
    )(xb, comb, wg, wu)


def _shared_up_kernel(x_ref, wg_ref, wu_ref, h_ref):
    h_ref[...] = _swiglu(x_ref[...], wg_ref[...], wu_ref[...]).astype(h_ref.dtype)


def shared_up(xb, wg, wu, *, layer, tm, tf, name):
    t, d = xb.shape
    f = wg.shape[2]
    tm, tf = min(tm, t), min(tf, f)
    assert t % tm == 0 and f % tf == 0
    vmem = 2 * (_nbytes((tm, d), BF16) + 2 * _nbytes((d, tf), BF16) + _nbytes((tm, tf), BF16))
    vmem += 4 * _nbytes((tm, tf), F32)
    return pl.pallas_call(
        _shared_up_kernel,
        grid=(f // tf, t // tm),
        in_specs=[pl.BlockSpec((tm, d), lambda j, i: (i, 0)),
                  pl.BlockSpec((None, d, tf), lambda j, i: (layer, 0, j)),
                  pl.BlockSpec((None, d, tf), lambda j, i: (layer, 0, j))],
        out_specs=pl.BlockSpec((tm, tf), lambda j, i: (i, j)),
        out_shape=jax.ShapeDtypeStruct((t, f), BF16),
        compiler_params=_cparams(2, vmem),
        name=name,
    )(xb, wg, wu)


def kernel(x, w_in, b_gates, conv_qk, mlstm_norm_g, sgu_norm_g, sgu_norm_b, w_spatial, b_spatial, w_out,
           ln1_g, ln1_b, w_router, b_router, w_exp_gate, w_exp_up, w_exp_down, w_sh_gate, w_sh_up, w_sh_down,
           ln2_g, ln2_b):
    bsz, seq_len, d = x.shape
    depth = w_in.shape[0]
    t = bsz * seq_len
    nh, dh = MLSTM_HEADS, MLSTM_HEAD_DIM
    w_m = nh * dh
    w_s = SGU_GROUPS * SGU_GROUP_DIM
    off_g = 4 * w_m
    n_e, f_e = w_exp_gate.shape[1], w_exp_gate.shape[3]
    nc_total = t // CHUNK

    xf = x.reshape(t, d)
    xb = xf.astype(BF16)
    lane_pad = V7X_LANES - N_EXPERTS

    w_in_t = jnp.swapaxes(w_in, 1, 2)
    w_usv_t = w_in_t[:, off_g + N_GATES:, :].astype(BF16)
    w_gate_t = jnp.pad(w_in_t[:, off_g:off_g + N_GATES, :], ((0, 0), (0, V7X_LANES - N_GATES), (0, 0))).astype(BF16)
    w_r = jnp.pad(w_router, ((0, 0), (0, 0), (0, lane_pad))).astype(BF16)
    b_r = jnp.pad(b_router, ((0, 0), (0, lane_pad))).reshape(depth, 1, V7X_LANES)
    wg_e, wu_e = w_exp_gate.astype(BF16), w_exp_up.astype(BF16)
    wd_e = w_exp_down.reshape(depth, n_e * f_e, d).astype(BF16)
    wg_s, wu_s, wd_s = w_sh_gate.astype(BF16), w_sh_up.astype(BF16), w_sh_down.astype(BF16)

    for l in range(depth):
        z_m = matmul(xb, w_in_t, tm=1024, tn=1024, out_dtype=BF16, n_cols=off_g, layer=l, rhs_t=True, slab=dh,
                     name=f"in_proj_m_{l}")
        z_s = matmul(xb, w_usv_t, tm=1024, tn=1024, out_dtype=BF16, layer=l, rhs_t=True, name=f"in_proj_s_{l}")
        g_raw = matmul(xb, w_gate_t, tm=1024, tn=V7X_LANES, out_dtype=F32, layer=l, rhs_t=True,
                       name=f"gate_proj_{l}")
        g4 = g_raw[:, :N_GATES].T.reshape(4, nh, t)
        tables = gate_prep(g4, b_gates[l].reshape(4, nh, 1), tl=2048, name=f"gate_prep_{l}")
        rows = tables.transpose(1, 0, 2).reshape(nh, 8, nc_total, CHUNK).transpose(0, 2, 1, 3)
        q_c, kt_c = conv_silu(z_m, conv_qk[l], n_slabs=nh, seq_len=seq_len, k_scale=dh ** -0.5, tr=1024,
                              name=f"conv_silu_{l}")
        h_m = mlstm(q_c, kt_c, z_m, rows, mlstm_norm_g[l], batch=bsz, seq_len=seq_len, name=f"mlstm_{l}")
        h_s = spatial_gating(z_s, sgu_norm_g[l], sgu_norm_b[l], w_spatial[l], b_spatial[l],
                             u_col=0, v_col=w_s // SGU_GROUP_DIM, tr=1024, name=f"sgu_{l}")
        mix = matmul_cat(h_m, h_s, w_out, w_out, tm=1024, tn=1024, out_dtype=BF16, layer=l, row_blocks=(0, 1),
                         name=f"out_proj_{l}")
        x1, x1b, comb = add_layer_norm(xf, mix, ln1_g[l], ln1_b[l], tm=256, router=(w_r[l], b_r[l]),
                                       name=f"ln1_{l}")
        hid_r = routed_up(x1b, comb, wg_e, wu_e, layer=l, tm=1024, per=4, name=f"routed_up_{l}")
        hid_s = shared_up(x1b, wg_s, wu_s, layer=l, tm=1024, tf=512, name=f"shared_up_{l}")
        y = matmul_cat(hid_r, hid_s, wd_e, wd_s, tm=512, tn=1024, out_dtype=BF16, layer=l, single=True,
                       name=f"ffn_down_{l}")
        xf, xb, _ = add_layer_norm(x1, y, ln2_g[l], ln2_b[l], tm=256, bf16_copy=l + 1 < depth, name=f"ln2_{l}")

    return xf.reshape(bsz, seq_len, d)
```

```python
import functools
import math

import jax
import jax.numpy as jnp
from jax import lax
from jax.experimental import pallas as pl
from jax.experimental.pallas import tpu as pltpu

MLSTM_HEADS = 8
MLSTM_HEAD_DIM = 256
CHUNK = 256
SGU_CHUNK = 128
SGU_GROUPS = 8
SGU_GROUP_DIM = 256
N_GATES = 4 * MLSTM_HEADS
N_EXPERTS = 64
TOP_K = 8
D_EXPERT = 128
ROUTE_SCALE = 2.5
DEPTH_FOR_NORM = 4
DEEPNORM_ALPHA = (2 * DEPTH_FOR_NORM) ** 0.25
LN_EPS = 1e-5

V7X_LANES = 128
V7X_VMEM_LIMIT_BYTES = 60000 * 1024

F32 = jnp.float32
BF16 = jnp.bfloat16


def _cparams(n_axes, vmem_bytes):
    limit = int(min(max(vmem_bytes * 5 // 4 + (4 << 20), 16 << 20), V7X_VMEM_LIMIT_BYTES))
    return pltpu.CompilerParams(dimension_semantics=("arbitrary",) * n_axes, vmem_limit_bytes=limit)


def _nbytes(shape, dtype):
    return math.prod(shape) * jnp.dtype(dtype).itemsize


def _weight_spec(w, rows, tn, layer, row_block=0, transposed=False, single=False):
    mode = dict(pipeline_mode=pl.Buffered(1)) if (single or w.dtype != BF16) else {}
    shape = (tn, rows) if transposed else (rows, tn)
    index = (lambda j: (j, row_block)) if transposed else (lambda j: (row_block, j))
    if w.ndim == 2:
        return pl.BlockSpec(shape, lambda j, i: index(j), **mode)
    return pl.BlockSpec((None,) + shape, lambda j, i: (layer,) + index(j), **mode)


_NT = (((1,), (1,)), ((), ()))


def _mm_kernel(a_ref, b_ref, o_ref, *scratch, rhs_t):
    if scratch:
        (wb_ref,) = scratch

        @pl.when(pl.program_id(1) == 0)
        def _():
            wb_ref[...] = b_ref[...].astype(BF16)

        b = wb_ref[...]
    else:
        b = b_ref[...]
    if rhs_t:
        acc = lax.dot_general(a_ref[...], b, _NT, preferred_element_type=F32)
    else:
        acc = jnp.dot(a_ref[...], b, preferred_element_type=F32)
    if len(o_ref.shape) == 2:
        o_ref[...] = acc.astype(o_ref.dtype)
    else:
        w = o_ref.shape[2]
        for c in range(o_ref.shape[0]):
            o_ref[c] = acc[:, c * w:(c + 1) * w].astype(o_ref.dtype)


def matmul(a, b, *, tm, tn, out_dtype, name, n_cols=None, layer=None, rhs_t=False, slab=None):
    m, k = a.shape
    n = (b.shape[-2] if rhs_t else b.shape[-1]) if n_cols is None else n_cols
    tm, tn = min(tm, m), min(tn, n)
    assert m % tm == 0 and n % tn == 0
    cast = b.dtype != BF16
    vmem = 2 * (_nbytes((tm, k), a.dtype) + _nbytes((tm, tn), out_dtype)) + _nbytes((tm, tn), F32)
    vmem += _nbytes((k, tn), F32) + _nbytes((k, tn), BF16) if cast else 2 * _nbytes((k, tn), BF16)
    if slab is None:
        out_spec = pl.BlockSpec((tm, tn), lambda j, i: (i, j))
        out_shape = jax.ShapeDtypeStruct((m, n), out_dtype)
    else:
        assert tn % slab == 0
        out_spec = pl.BlockSpec((tn // slab, tm, slab), lambda j, i: (j, i, 0))
        out_shape = jax.ShapeDtypeStruct((n // slab, m, slab), out_dtype)
    return pl.pallas_call(
        functools.partial(_mm_kernel, rhs_t=rhs_t),
        grid=(n // tn, m // tm),
        in_specs=[pl.BlockSpec((tm, k), lambda j, i: (i, 0)), _weight_spec(b, k, tn, layer, transposed=rhs_t)],
        out_specs=out_spec,
        out_shape=out_shape,
        scratch_shapes=[pltpu.VMEM((tn, k) if rhs_t else (k, tn), BF16)] if cast else [],
        compiler_params=_cparams(2, vmem),
        name=name,
    )(a, b)


def _mm2_kernel(a1_ref, a2_ref, b1_ref, b2_ref, o_ref, *scratch):
    if scratch:
        w1_ref, w2_ref = scratch

        @pl.when(pl.program_id(1) == 0)
        def _():
            w1_ref[...] = b1_ref[...].astype(BF16)
            w2_ref[...] = b2_ref[...].astype(BF16)

        b1, b2 = w1_ref[...], w2_ref[...]
    else:
        b1, b2 = b1_ref[...], b2_ref[...]
    acc = jnp.dot(a1_ref[...], b1, preferred_element_type=F32)
    acc = acc + jnp.dot(a2_ref[...], b2, preferred_element_type=F32)
    o_ref[...] = acc.astype(o_ref.dtype)


def matmul_cat(a1, a2, b1, b2, *, tm, tn, out_dtype, name, layer=None, row_blocks=(0, 0), single=False):
    m, k1 = a1.shape
    _, k2 = a2.shape
    n = b1.shape[-1]
    tm, tn = min(tm, m), min(tn, n)
    assert m % tm == 0 and n % tn == 0
    k = k1 + k2
    assert row_blocks[0] == row_blocks[1] or k1 == k2
    cast = b1.dtype != BF16
    vmem = 2 * (_nbytes((tm, k), a1.dtype) + _nbytes((tm, tn), out_dtype)) + _nbytes((tm, tn), F32)
    vmem += _nbytes((k, tn), F32) + _nbytes((k, tn), BF16) if cast else (1 if single else 2) * _nbytes((k, tn), BF16)
    return pl.pallas_call(
        _mm2_kernel,
        grid=(n // tn, m // tm),
        in_specs=[pl.BlockSpec((tm, k1), lambda j, i: (i, 0)),
                  pl.BlockSpec((tm, k2), lambda j, i: (i, 0)),
                  _weight_spec(b1, k1, tn, layer, row_blocks[0], single=single),
                  _weight_spec(b2, k2, tn, layer, row_blocks[1], single=single)],
        out_specs=pl.BlockSpec((tm, tn), lambda j, i: (i, j)),
        out_shape=jax.ShapeDtypeStruct((m, n), out_dtype),
        scratch_shapes=[pltpu.VMEM((k1, tn), BF16), pltpu.VMEM((k2, tn), BF16)] if cast else [],
        compiler_params=_cparams(2, vmem),
        name=name,
    )(a1, a2, b1, b2)


def _top_k_gate(logits, bias):
    lane = lax.broadcasted_iota(jnp.int32, logits.shape, 1)
    scores = jax.nn.sigmoid(logits)
    sel = jnp.where(lane < N_EXPERTS, scores + bias, -jnp.inf)
    picked = jnp.zeros(logits.shape, jnp.bool_)
    for _ in range(TOP_K):
        best = jnp.max(sel, axis=1, keepdims=True)
        first = jnp.min(jnp.where(sel == best, lane, V7X_LANES), axis=1, keepdims=True)
        hit = lane == first
        picked = jnp.logical_or(picked, hit)
        sel = jnp.where(hit, -jnp.inf, sel)
    w = jnp.where(picked, scores, 0.0)
    return w / jnp.sum(w, axis=1, keepdims=True) * ROUTE_SCALE


def _add_ln_kernel(x_ref, r_ref, g_ref, b_ref, *rest, bf16_copy, route, project):
    rest = list(rest)
    if route:
        wr_ref, br_ref = rest.pop(0), rest.pop(0)
    if project:
        wp_ref = rest.pop(0)
    outs = rest
    v = DEEPNORM_ALPHA * x_ref[...] + r_ref[...].astype(F32)
    mu = jnp.mean(v, axis=-1, keepdims=True)
    c = v - mu
    var = jnp.mean(c * c, axis=-1, keepdims=True)
    y = c * lax.rsqrt(var + LN_EPS) * g_ref[...] + b_ref[...]
    outs[0][...] = y
    if bf16_copy:
        yb = y.astype(BF16)
        outs[1][...] = yb
        if route:
            logits = jnp.dot(yb, wr_ref[...], preferred_element_type=F32)
            outs[2][...] = _top_k_gate(logits, br_ref[...])
        if project:
            outs[-1][...] = lax.dot_general(yb, wp_ref[...], _NT, preferred_element_type=F32)


def add_layer_norm(x, r, g, b, *, tm, name, bf16_copy=True, router=None, proj_t=None):
    t, d = x.shape
    tm = min(tm, t)
    assert t % tm == 0 and ((router is None and proj_t is None) or bf16_copy) and (router is None or proj_t is None)
    vmem = 2 * (3 * _nbytes((tm, d), F32) + _nbytes((tm, d), BF16)) + 4 * _nbytes((tm, d), F32)
    row = lambda i: (i, 0)
    fixed = lambda i: (0, 0)
    in_specs = [pl.BlockSpec((tm, d), row), pl.BlockSpec((tm, d), row),
                pl.BlockSpec((1, d), fixed), pl.BlockSpec((1, d), fixed)]
    operands = [x, r, g.reshape(1, d), b.reshape(1, d)]
    out_specs = [pl.BlockSpec((tm, d), row)]
    out_shape = [jax.ShapeDtypeStruct((t, d), F32)]
    if bf16_copy:
        out_specs.append(pl.BlockSpec((tm, d), row))
        out_shape.append(jax.ShapeDtypeStruct((t, d), BF16))
    if router is not None:
        in_specs += [pl.BlockSpec((d, V7X_LANES), fixed), pl.BlockSpec((1, V7X_LANES), fixed)]
        operands += list(router)
        out_specs.append(pl.BlockSpec((tm, V7X_LANES), row))
        out_shape.append(jax.ShapeDtypeStruct((t, V7X_LANES), F32))
        vmem += 2 * _nbytes((d, V7X_LANES), BF16) + 10 * _nbytes((tm, V7X_LANES), F32)
    if proj_t is not None:
        in_specs.append(pl.BlockSpec((V7X_LANES, d), fixed))
        operands.append(proj_t)
        out_specs.append(pl.BlockSpec((tm, V7X_LANES), row))
        out_shape.append(jax.ShapeDtypeStruct((t, V7X_LANES), F32))
        vmem += 2 * _nbytes((d, V7X_LANES), BF16) + 2 * _nbytes((tm, V7X_LANES), F32)
    out = pl.pallas_call(
        functools.partial(_add_ln_kernel, bf16_copy=bf16_copy, route=router is not None, project=proj_t is not None),
        grid=(t // tm,),
        in_specs=in_specs,
        out_specs=out_specs,
        out_shape=out_shape,
        compiler_params=_cparams(1, vmem),
        name=name,
    )(*operands)
    return out[0], (out[1] if bf16_copy else None), (out[2] if (router is not None or proj_t is not None) else None)


_HALO = 16


def _conv3_silu(z_ref, zp_ref, zn_ref, w_ref, seq_len):
    tr = z_ref.shape[0]
    x = z_ref[...].astype(F32)
    row = lax.broadcasted_iota(jnp.int32, x.shape, 0)
    t0 = pl.program_id(0) * tr
    prev_row = jnp.where(lax.rem(t0, seq_len) == 0, 0.0, zp_ref[_HALO - 1:_HALO, :].astype(F32))
    next_row = jnp.where(lax.rem(t0 + tr, seq_len) == 0, 0.0, zn_ref[0:1, :].astype(F32))
    x_prev = jnp.where(row == 0, prev_row, pltpu.roll(x, 1, axis=0))
    x_next = jnp.where(row == tr - 1, next_row, pltpu.roll(x, tr - 1, axis=0))
    y = w_ref[0:1, :] * x_prev + w_ref[1:2, :] * x + w_ref[2:3, :] * x_next
    return y * jax.nn.sigmoid(y)


def _conv_silu_kernel(q_ref, qp_ref, qn_ref, k_ref, kp_ref, kn_ref, wq_ref, wk_ref, oq_ref, okt_ref,
                      *, seq_len, k_scale):
    oq_ref[...] = _conv3_silu(q_ref, qp_ref, qn_ref, wq_ref, seq_len).astype(oq_ref.dtype)
    k = _conv3_silu(k_ref, kp_ref, kn_ref, wk_ref, seq_len) * k_scale
    for c in range(okt_ref.shape[0]):
        okt_ref[c] = k[c * CHUNK:(c + 1) * CHUNK, :].T.astype(okt_ref.dtype)


def conv_silu(z3, conv_w, *, n_slabs, seq_len, k_scale, tr, name):
    _, t, tc = z3.shape
    tr = min(tr, seq_len)
    assert seq_len % tr == 0 and tr % CHUNK == 0
    hb = tr // _HALO
    n_hb = t // _HALO
    vmem = 2 * (3 * _nbytes((tr, tc), BF16) + 4 * _nbytes((_HALO, tc), BF16) + _nbytes((tr, tc), BF16))
    vmem += 12 * _nbytes((tr, tc), F32)
    cur = lambda off: (lambda i, j: (j + off, i, 0))
    prev = lambda off: (lambda i, j: (j + off, jnp.maximum(i * hb - 1, 0), 0))
    nxt = lambda off: (lambda i, j: (j + off, jnp.minimum((i + 1) * hb, n_hb - 1), 0))
    return pl.pallas_call(
        functools.partial(_conv_silu_kernel, seq_len=seq_len, k_scale=k_scale),
        grid=(t // tr, n_slabs),
        in_specs=[pl.BlockSpec((None, tr, tc), cur(0)),
                  pl.BlockSpec((None, _HALO, tc), prev(0)),
                  pl.BlockSpec((None, _HALO, tc), nxt(0)),
                  pl.BlockSpec((None, tr, tc), cur(n_slabs)),
                  pl.BlockSpec((None, _HALO, tc), prev(n_slabs)),
                  pl.BlockSpec((None, _HALO, tc), nxt(n_slabs)),
                  pl.BlockSpec((3, tc), lambda i, j: (0, j)),
                  pl.BlockSpec((3, tc), lambda i, j: (0, j + n_slabs))],
        out_specs=[pl.BlockSpec((None, tr, tc), lambda i, j: (j, i, 0)),
                   pl.BlockSpec((tr // CHUNK, tc, CHUNK), lambda i, j: (i, j, 0))],
        out_shape=[jax.ShapeDtypeStruct((n_slabs, t, tc), BF16),
                   jax.ShapeDtypeStruct((t // CHUNK, n_slabs * tc, CHUNK), BF16)],
        compiler_params=_cparams(2, vmem),
        name=name,
    )(z3, z3, z3, z3, z3, z3, conv_w, conv_w)


def _chunk_scan(x, lane_in_chunk, *, suffix):
    n = x.shape[-1]
    k = 1
    while k < CHUNK:
        if suffix:
            shifted = pltpu.roll(x, n - k, axis=1)
            x = x + jnp.where(lane_in_chunk < CHUNK - k, shifted, 0.0)
        else:
            shifted = pltpu.roll(x, k, axis=1)
            x = x + jnp.where(lane_in_chunk >= k, shifted, 0.0)
        k *= 2
    return x


def _gate_prep_kernel(g_ref, bias_ref, o_ref):
    i_f = g_ref[0] + bias_ref[0]
    lf_f = jax.nn.log_sigmoid(g_ref[1] + bias_ref[1])
    i_b = g_ref[2] + bias_ref[2]
    lf_b = jax.nn.log_sigmoid(g_ref[3] + bias_ref[3])
    lane = lax.rem(lax.broadcasted_iota(jnp.int32, i_f.shape, 1), CHUNK)
    b_f = _chunk_scan(lf_f, lane, suffix=False)
    g_f = b_f + _chunk_scan(lf_f, lane, suffix=True) - lf_f
    b_b = _chunk_scan(lf_b, lane, suffix=True)
    g_b = b_b + _chunk_scan(lf_b, lane, suffix=False) - lf_b
    r_f = i_f - b_f
    r_b = i_b - b_b
    o_ref[0] = b_f
    o_ref[1] = r_f
    o_ref[2] = g_f + r_f
    o_ref[3] = g_f
    o_ref[4] = b_b
    o_ref[5] = r_b
    o_ref[6] = g_b + r_b
    o_ref[7] = g_b


def gate_prep(g4, bias, *, tl, name):
    _, h, t = g4.shape
    tl = min(tl, t)
    assert t % tl == 0 and tl % CHUNK == 0
    vmem = 2 * (_nbytes((4, h, tl), F32) + _nbytes((8, h, tl), F32)) + 16 * _nbytes((h, tl), F32)
    return pl.pallas_call(
        _gate_prep_kernel,
        grid=(t // tl,),
        in_specs=[pl.BlockSpec((4, h, tl), lambda i: (0, 0, i)),
                  pl.BlockSpec((4, h, 1), lambda i: (0, 0, 0))],
        out_specs=pl.BlockSpec((8, h, tl), lambda i: (0, 0, i)),
        out_shape=jax.ShapeDtypeStruct((8, h, t), F32),
        compiler_params=_cparams(1, vmem),
        name=name,
    )(g4, bias)


def _lanes(x, n):
    return x if n == V7X_LANES else jnp.concatenate([x] * (n // V7X_LANES), axis=1)


def _mlstm_chunk(q, kt, v_ext, b_row, r_row, a_row, st_ref, m11, mask, eye):
    dv = v_ext.shape[1] - V7X_LANES
    b_col = jnp.sum(jnp.where(eye, b_row, 0.0), axis=1, keepdims=True)
    b_wide = jnp.broadcast_to(b_col, (b_col.shape[0], V7X_LANES))
    n_src = r_row.shape[1]
    dmat = jnp.where(mask, _lanes(b_wide, n_src) + r_row, -jnp.inf)
    inter = b_wide + m11
    m_t = jnp.maximum(inter, jnp.max(dmat, axis=1, keepdims=True))
    w_intra = jnp.exp(dmat - _lanes(m_t, n_src))
    w_inter = jnp.exp(inter - m_t)
    s_qk = jnp.dot(q, kt, preferred_element_type=F32) * w_intra
    intra = jnp.dot(s_qk.astype(BF16), v_ext, preferred_element_type=F32)
    state = st_ref[...]
    carried = jnp.dot(q, state.astype(BF16), preferred_element_type=F32)
    num = intra[:, :dv] + _lanes(w_inter, dv) * carried[:, :dv]
    den = intra[:, dv:] + w_inter * carried[:, dv:]
    h = num / _lanes(jnp.maximum(jnp.abs(den), jnp.exp(-m_t)), dv)
    g11 = a_row[:, 0:1] - r_row[:, 0:1]
    m_new = jnp.maximum(g11 + m11, jnp.max(a_row, axis=1, keepdims=True))
    wa = jnp.exp(a_row - m_new)
    decay = jnp.exp(g11 + m11 - m_new)
    ktw = (kt.astype(F32) * wa).astype(BF16)
    st_ref[...] = decay * state + jnp.dot(ktw, v_ext, preferred_element_type=F32)
    return h, m_new


def _mlstm_kernel(q_ref, kt_ref, v_ref, o_ref, rows_ref, hg_ref, out_ref, acc_ref, stf_ref, stb_ref):
    nc = rows_ref.shape[0]
    L = CHUNK
    stf_ref[...] = jnp.zeros_like(stf_ref)
    stb_ref[...] = jnp.zeros_like(stb_ref)
    t_idx = lax.broadcasted_iota(jnp.int32, (L, L), 0)
    s_idx = lax.broadcasted_iota(jnp.int32, (L, L), 1)
    causal = s_idx <= t_idx
    anti = s_idx >= t_idx
    eye = s_idx == t_idx
    ones = jnp.ones((L, V7X_LANES), BF16)

    def directions(c, m_f, m_b):
        cb = nc - 1 - c
        sf = pl.ds(pl.multiple_of(c * L, L), L)
        sb = pl.ds(pl.multiple_of(cb * L, L), L)
        rows_f = rows_ref[c]
        rows_b = rows_ref[cb]
        h_f, m_f = _mlstm_chunk(q_ref[sf, :], kt_ref[c], jnp.concatenate([v_ref[sf, :], ones], axis=1),
                                rows_f[0:1, :], rows_f[1:2, :], rows_f[2:3, :], stf_ref, m_f, causal, eye)
        h_b, m_b = _mlstm_chunk(q_ref[sb, :], kt_ref[cb], jnp.concatenate([v_ref[sb, :], ones], axis=1),
                                rows_b[4:5, :], rows_b[5:6, :], rows_b[6:7, :], stb_ref, m_b, anti, eye)
        return sf, sb, h_f, h_b, m_f, m_b

    def first_half(c, carry):
        sf, sb, h_f, h_b, m_f, m_b = directions(c, *carry)
        acc_ref[sf, :] = h_f
        acc_ref[sb, :] = h_b
        return m_f, m_b

    def finish(sl, h_new):
        hh = (acc_ref[sl, :] + h_new) * jax.nn.sigmoid(o_ref[sl, :].astype(F32))
        mu = jnp.mean(hh, axis=1, keepdims=True)
        cen = hh - mu
        var = jnp.mean(cen * cen, axis=1, keepdims=True)
        out_ref[sl, :] = (cen * lax.rsqrt(var + LN_EPS) * hg_ref[...]).astype(out_ref.dtype)

    def second_half(c, carry):
        sf, sb, h_f, h_b, m_f, m_b = directions(c, *carry)
        finish(sf, h_f)
        finish(sb, h_b)
        return m_f, m_b

    zero = jnp.zeros((1, 1), F32)
    carry = lax.fori_loop(0, nc // 2, first_half, (zero, zero))
    lax.fori_loop(nc // 2, nc, second_half, carry)


def mlstm(q3, kt, z3, rows, head_g, *, batch, seq_len, name):
    t = q3.shape[1]
    nh, dh = MLSTM_HEADS, MLSTM_HEAD_DIM
    nc = seq_len // CHUNK
    assert nc % 2 == 0
    vmem = 2 * 4 * _nbytes((seq_len, dh), BF16)
    vmem += 2 * _nbytes((seq_len, dh), BF16) + _nbytes((seq_len, dh), F32) + 2 * _nbytes((nc, 8, CHUNK), F32)
    vmem += 4 * _nbytes((dh, dh + V7X_LANES), F32)
    return pl.pallas_call(
        _mlstm_kernel,
        grid=(batch, nh),
        in_specs=[pl.BlockSpec((None, seq_len, dh), lambda b, h: (h, b, 0)),
                  pl.BlockSpec((nc, dh, CHUNK), lambda b, h: (b, h, 0)),
                  pl.BlockSpec((None, seq_len, dh), lambda b, h: (2 * nh + h, b, 0)),
                  pl.BlockSpec((None, seq_len, dh), lambda b, h: (3 * nh + h, b, 0)),
                  pl.BlockSpec((None, nc, 8, CHUNK), lambda b, h: (h, b, 0, 0)),
                  pl.BlockSpec((None, 1, dh), lambda b, h: (h, 0, 0))],
        out_specs=pl.BlockSpec((seq_len, dh), lambda b, h: (b, h)),
        out_shape=jax.ShapeDtypeStruct((t, nh * dh), BF16),
        scratch_shapes=[pltpu.VMEM((seq_len, dh), F32),
                        pltpu.VMEM((dh, dh + V7X_LANES), F32), pltpu.VMEM((dh, dh + V7X_LANES), F32)],
        compiler_params=_cparams(2, vmem),
        name=name,
    )(q3, kt, z3, z3, rows, head_g.reshape(nh, 1, dh))


def _gelu_tanh(x):
    return 0.5 * x * (1.0 + jnp.tanh(math.sqrt(2.0 / math.pi) * (x + 0.044715 * (x * x * x))))


def _sgu_kernel(u_ref, v_ref, ng_ref, nb_ref, ws_ref, bs_ref, o_ref):
    n_chunks = u_ref.shape[0] // SGU_CHUNK
    ws = ws_ref[...]
    for c in range(n_chunks):
        sl = slice(c * SGU_CHUNK, (c + 1) * SGU_CHUNK)
        v = _gelu_tanh(v_ref[sl, :].astype(F32))
        mu = jnp.mean(v, axis=1, keepdims=True)
        cen = v - mu
        var = jnp.mean(cen * cen, axis=1, keepdims=True)
        vn = cen * lax.rsqrt(var + LN_EPS) * ng_ref[...] + nb_ref[...]
        sp = jnp.dot(ws, vn.astype(BF16), preferred_element_type=F32) + bs_ref[...]
        o_ref[sl, :] = (_gelu_tanh(u_ref[sl, :].astype(F32)) * sp).astype(o_ref.dtype)


def spatial_gating(z, norm_g, norm_b, w_s, b_s, *, u_col, v_col, tr, name):
    t = z.shape[0]
    g, dg = SGU_GROUPS, SGU_GROUP_DIM
    tr = min(tr, t)
    assert t % tr == 0 and tr % SGU_CHUNK == 0
    vmem = 2 * 3 * _nbytes((tr, dg), BF16) + 8 * _nbytes((SGU_CHUNK, dg), F32) * (tr // SGU_CHUNK)
    return pl.pallas_call(
        _sgu_kernel,
        grid=(g, t // tr),
        in_specs=[pl.BlockSpec((tr, dg), lambda j, i: (i, u_col + j)),
                  pl.BlockSpec((tr, dg), lambda j, i: (i, v_col + j)),
                  pl.BlockSpec((None, 1, dg), lambda j, i: (j, 0, 0)),
                  pl.BlockSpec((None, 1, dg), lambda j, i: (j, 0, 0)),
                  pl.BlockSpec((None, SGU_CHUNK, SGU_CHUNK), lambda j, i: (j, 0, 0)),
                  pl.BlockSpec((None, SGU_CHUNK, 1), lambda j, i: (j, 0, 0))],
        out_specs=pl.BlockSpec((tr, dg), lambda j, i: (i, j)),
        out_shape=jax.ShapeDtypeStruct((t, g * dg), BF16),
        compiler_params=_cparams(2, vmem),
        name=name,
    )(z, z, norm_g.reshape(g, 1, dg), norm_b.reshape(g, 1, dg), w_s.astype(BF16), b_s.reshape(g, SGU_CHUNK, 1))


def _swiglu(x, wg, wu):
    gate = jnp.dot(x, wg, preferred_element_type=F32)
    up = jnp.dot(x, wu, preferred_element_type=F32)
    return gate * jax.nn.sigmoid(gate) * up


def _routed_up_kernel(x_ref, comb_ref, wg_ref, wu_ref, h_ref, wgc_ref, wuc_ref):
    per = wg_ref.shape[0]
    fe = wg_ref.shape[2]
    j = pl.program_id(0)

    @pl.when(pl.program_id(1) == 0)
    def _():
        for e in range(per):
            wgc_ref[:, e * fe:(e + 1) * fe] = wg_ref[e]
            wuc_ref[:, e * fe:(e + 1) * fe] = wu_ref[e]

    h = _swiglu(x_ref[...], wgc_ref[...], wuc_ref[...])
    e_row = lax.broadcasted_iota(jnp.int32, (V7X_LANES, per * fe), 0)
    e_col = lax.broadcasted_iota(jnp.int32, (V7X_LANES, per * fe), 1) // fe + j * per
    expand = jnp.where(e_row == e_col, 1.0, 0.0).astype(BF16)
    comb = comb_ref[...]
    c_hi = comb.astype(BF16)
    c_lo = (comb - c_hi.astype(F32)).astype(BF16)
    scale = jnp.dot(jnp.concatenate([c_hi, c_lo], axis=1), jnp.concatenate([expand, expand], axis=0),
                    preferred_element_type=F32)
    h_ref[...] = (h * scale).astype(h_ref.dtype)


def routed_up(xb, comb, wg, wu, *, layer, tm, per, name):
    t, d = xb.shape
    _, n_e, _, fe = wg.shape
    tm = min(tm, t)
    assert t % tm == 0 and n_e % per == 0
    tf = per * fe
    vmem = 2 * (_nbytes((tm, d), BF16) + _nbytes((tm, V7X_LANES), F32) + 2 * _nbytes((d, tf), BF16))
    vmem += 2 * _nbytes((tm, tf), BF16) + 2 * _nbytes((d, tf), BF16) + 5 * _nbytes((tm, tf), F32)
    return pl.pallas_call(
        _routed_up_kernel,
        grid=(n_e // per, t // tm),
        in_specs=[pl.BlockSpec((tm, d), lambda j, i: (i, 0)),
                  pl.BlockSpec((tm, V7X_LANES), lambda j, i: (i, 0)),
                  pl.BlockSpec((None, per, d, fe), lambda j, i: (layer, j, 0, 0)),
                  pl.BlockSpec((None, per, d, fe), lambda j, i: (layer, j, 0, 0))],
        out_specs=pl.BlockSpec((tm, tf), lambda j, i: (i, j)),
        out_shape=jax.ShapeDtypeStruct((t, n_e * fe), BF16),
        scratch_shapes=[pltpu.VMEM((d, tf), BF16), pltpu.VMEM((d, tf), BF16)],
        compiler_params=_cparams(2, vmem),
        name=name,
    )(xb, comb, wg, wu)


def _shared_up_kernel(x_ref, wg_ref, wu_ref, h_ref):
    h_ref[...] = _swiglu(x_ref[...], wg_ref[...], wu_ref[...]).astype(h_ref.dtype)


def shared_up(xb, wg, wu, *, layer, tm, tf, name):
    t, d = xb.shape
    f = wg.shape[2]
    tm, tf = min(tm, t), min(tf, f)
    assert t % tm == 0 and f % tf == 0
    vmem = 2 * (_nbytes((tm, d), BF16) + 2 * _nbytes((d, tf), BF16) + _nbytes((tm, tf), BF16))
    vmem += 4 * _nbytes((tm, tf), F32)
    return pl.pallas_call(
        _shared_up_kernel,
        grid=(f // tf, t // tm),
        in_specs=[pl.BlockSpec((tm, d), lambda j, i: (i, 0)),
                  pl.BlockSpec((None, d, tf), lambda j, i: (layer, 0, j)),
                  pl.BlockSpec((None, d, tf), lambda j, i: (layer, 0, j))],
        out_specs=pl.BlockSpec((tm, tf), lambda j, i: (i, j)),
        out_shape=jax.ShapeDtypeStruct((t, f), BF16),
        compiler_params=_cparams(2, vmem),
        name=name,
    )(xb, wg, wu)


def kernel(x, w_in, b_gates, conv_qk, mlstm_norm_g, sgu_norm_g, sgu_norm_b, w_spatial, b_spatial, w_out,
           ln1_g, ln1_b, w_router, b_router, w_exp_gate, w_exp_up, w_exp_down, w_sh_gate, w_sh_up, w_sh_down,
           ln2_g, ln2_b):
    bsz, seq_len, d = x.shape
    depth = w_in.shape[0]
    t = bsz * seq_len
    nh, dh = MLSTM_HEADS, MLSTM_HEAD_DIM
    w_m = nh * dh
    w_s = SGU_GROUPS * SGU_GROUP_DIM
    off_g = 4 * w_m
    n_e, f_e = w_exp_gate.shape[1], w_exp_gate.shape[3]
    nc_total = t // CHUNK

    xf = x.reshape(t, d)
    xb = xf.astype(BF16)
    lane_pad = V7X_LANES - N_EXPERTS

    w_in_t = jnp.swapaxes(w_in, 1, 2)
    w_usv_t = w_in_t[:, off_g + N_GATES:, :]
    w_gate_t = jnp.pad(w_in_t[:, off_g:off_g + N_GATES, :], ((0, 0), (0, V7X_LANES - N_GATES), (0, 0))).astype(BF16)
    w_r = jnp.pad(w_router, ((0, 0), (0, 0), (0, lane_pad))).astype(BF16)
    b_r = jnp.pad(b_router, ((0, 0), (0, lane_pad))).reshape(depth, 1, V7X_LANES)
    wg_e, wu_e = w_exp_gate.astype(BF16), w_exp_up.astype(BF16)
    wd_e = w_exp_down.reshape(depth, n_e * f_e, d).astype(BF16)
    wg_s, wu_s, wd_s = w_sh_gate.astype(BF16), w_sh_up.astype(BF16), w_sh_down.astype(BF16)

    for l in range(depth):
        z_m = matmul(xb, w_in_t, tm=1024, tn=1024, out_dtype=BF16, n_cols=off_g, layer=l, rhs_t=True, slab=dh,
                     name=f"in_proj_m_{l}")
        z_s = matmul(xb, w_usv_t, tm=1024, tn=1024, out_dtype=BF16, layer=l, rhs_t=True, name=f"in_proj_s_{l}")
        if l == 0:
            g_raw = matmul(xb, w_gate_t, tm=1024, tn=V7X_LANES, out_dtype=F32, layer=0, rhs_t=True, name="gate_proj_0")
        g4 = g_raw[:, :N_GATES].T.reshape(4, nh, t)
        tables = gate_prep(g4, b_gates[l].reshape(4, nh, 1), tl=2048, name=f"gate_prep_{l}")
        rows = tables.transpose(1, 0, 2).reshape(nh, 8, nc_total, CHUNK).transpose(0, 2, 1, 3)
        q_c, kt_c = conv_silu(z_m, conv_qk[l], n_slabs=nh, seq_len=seq_len, k_scale=dh ** -0.5, tr=1024,
                              name=f"conv_silu_{l}")
        h_m = mlstm(q_c, kt_c, z_m, rows, mlstm_norm_g[l], batch=bsz, seq_len=seq_len, name=f"mlstm_{l}")
        h_s = spatial_gating(z_s, sgu_norm_g[l], sgu_norm_b[l], w_spatial[l], b_spatial[l],
                             u_col=0, v_col=w_s // SGU_GROUP_DIM, tr=1024, name=f"sgu_{l}")
        mix = matmul_cat(h_m, h_s, w_out, w_out, tm=1024, tn=1024, out_dtype=BF16, layer=l, row_blocks=(0, 1),
                         name=f"out_proj_{l}")
        x1, x1b, comb = add_layer_norm(xf, mix, ln1_g[l], ln1_b[l], tm=256, router=(w_r[l], b_r[l]),
                                       name=f"ln1_{l}")
        hid_r = routed_up(x1b, comb, wg_e, wu_e, layer=l, tm=1024, per=4, name=f"routed_up_{l}")
        hid_s = shared_up(x1b, wg_s, wu_s, layer=l, tm=1024, tf=512, name=f"shared_up_{l}")
        y = matmul_cat(hid_r, hid_s, wd_e, wd_s, tm=512, tn=1024, out_dtype=BF16, layer=l, single=True,
                       name=f"ffn_down_{l}")
        last = l + 1 == depth
        xf, xb, g_raw = add_layer_norm(x1, y, ln2_g[l], ln2_b[l], tm=256, bf16_copy=not last,
                                       proj_t=None if last else w_gate_t[l + 1], name=f"ln2_{l}")

    return xf.reshape(bsz, seq_len, d)
```

```python
import functools
import math

import jax
import jax.numpy as jnp
from jax import lax
from jax.experimental import pallas as pl
from jax.experimental.pallas import tpu as pltpu

MLSTM_HEADS = 8
MLSTM_HEAD_DIM = 256
CHUNK = 256
SGU_CHUNK = 128
SGU_GROUPS = 8
SGU_GROUP_DIM = 256
N_GATES = 4 * MLSTM_HEADS
N_EXPERTS = 64
TOP_K = 8
D_EXPERT = 128
ROUTE_SCALE = 2.5
DEPTH_FOR_NORM = 4
DEEPNORM_ALPHA = (2 * DEPTH_FOR_NORM) ** 0.25
LN_EPS = 1e-5

V7X_LANES = 128
V7X_VMEM_LIMIT_BYTES = 60000 * 1024

F32 = jnp.float32
BF16 = jnp.bfloat16


def _cparams(n_axes, vmem_bytes):
    limit = int(min(max(vmem_bytes * 5 // 4 + (4 << 20), 16 << 20), V7X_VMEM_LIMIT_BYTES))
    return pltpu.CompilerParams(dimension_semantics=("arbitrary",) * n_axes, vmem_limit_bytes=limit)


def _nbytes(shape, dtype):
    return math.prod(shape) * jnp.dtype(dtype).itemsize


def _weight_spec(w, rows, tn, layer, row_block=0, transposed=False, single=False):
    mode = dict(pipeline_mode=pl.Buffered(1)) if (single or w.dtype != BF16) else {}
    shape = (tn, rows) if transposed else (rows, tn)
    index = (lambda j: (j, row_block)) if transposed else (lambda j: (row_block, j))
    if w.ndim == 2:
        return pl.BlockSpec(shape, lambda j, i: index(j), **mode)
    return pl.BlockSpec((None,) + shape, lambda j, i: (layer,) + index(j), **mode)


_NT = (((1,), (1,)), ((), ()))


def _mm_kernel(a_ref, b_ref, o_ref, *scratch, rhs_t):
    if scratch:
        (wb_ref,) = scratch

        @pl.when(pl.program_id(1) == 0)
        def _():
            wb_ref[...] = b_ref[...].astype(BF16)

        b = wb_ref[...]
    else:
        b = b_ref[...]
    if rhs_t:
        acc = lax.dot_general(a_ref[...], b, _NT, preferred_element_type=F32)
    else:
        acc = jnp.dot(a_ref[...], b, preferred_element_type=F32)
    if len(o_ref.shape) == 2:
        o_ref[...] = acc.astype(o_ref.dtype)
    else:
        w = o_ref.shape[2]
        for c in range(o_ref.shape[0]):
            o_ref[c] = acc[:, c * w:(c + 1) * w].astype(o_ref.dtype)


def matmul(a, b, *, tm, tn, out_dtype, name, n_cols=None, layer=None, rhs_t=False, slab=None):
    m, k = a.shape
    n = (b.shape[-2] if rhs_t else b.shape[-1]) if n_cols is None else n_cols
    tm, tn = min(tm, m), min(tn, n)
    assert m % tm == 0 and n % tn == 0
    cast = b.dtype != BF16
    vmem = 2 * (_nbytes((tm, k), a.dtype) + _nbytes((tm, tn), out_dtype)) + _nbytes((tm, tn), F32)
    vmem += _nbytes((k, tn), F32) + _nbytes((k, tn), BF16) if cast else 2 * _nbytes((k, tn), BF16)
    if slab is None:
        out_spec = pl.BlockSpec((tm, tn), lambda j, i: (i, j))
        out_shape = jax.ShapeDtypeStruct((m, n), out_dtype)
    else:
        assert tn % slab == 0
        out_spec = pl.BlockSpec((tn // slab, tm, slab), lambda j, i: (j, i, 0))
        out_shape = jax.ShapeDtypeStruct((n // slab, m, slab), out_dtype)
    return pl.pallas_call(
        functools.partial(_mm_kernel, rhs_t=rhs_t),
        grid=(n // tn, m // tm),
        in_specs=[pl.BlockSpec((tm, k), lambda j, i: (i, 0)), _weight_spec(b, k, tn, layer, transposed=rhs_t)],
        out_specs=out_spec,
        out_shape=out_shape,
        scratch_shapes=[pltpu.VMEM((tn, k) if rhs_t else (k, tn), BF16)] if cast else [],
        compiler_params=_cparams(2, vmem),
        name=name,
    )(a, b)


def _mm2_kernel(a1_ref, a2_ref, b1_ref, b2_ref, o_ref, *scratch):
    if scratch:
        w1_ref, w2_ref = scratch

        @pl.when(pl.program_id(1) == 0)
        def _():
            w1_ref[...] = b1_ref[...].astype(BF16)
            w2_ref[...] = b2_ref[...].astype(BF16)

        b1, b2 = w1_ref[...], w2_ref[...]
    else:
        b1, b2 = b1_ref[...], b2_ref[...]
    acc = jnp.dot(a1_ref[...], b1, preferred_element_type=F32)
    acc = acc + jnp.dot(a2_ref[...], b2, preferred_element_type=F32)
    o_ref[...] = acc.astype(o_ref.dtype)


def matmul_cat(a1, a2, b1, b2, *, tm, tn, out_dtype, name, layer=None, row_blocks=(0, 0), single=False):
    m, k1 = a1.shape
    _, k2 = a2.shape
    n = b1.shape[-1]
    tm, tn = min(tm, m), min(tn, n)
    assert m % tm == 0 and n % tn == 0
    k = k1 + k2
    assert row_blocks[0] == row_blocks[1] or k1 == k2
    cast = b1.dtype != BF16
    vmem = 2 * (_nbytes((tm, k), a1.dtype) + _nbytes((tm, tn), out_dtype)) + _nbytes((tm, tn), F32)
    vmem += _nbytes((k, tn), F32) + _nbytes((k, tn), BF16) if cast else (1 if single else 2) * _nbytes((k, tn), BF16)
    return pl.pallas_call(
        _mm2_kernel,
        grid=(n // tn, m // tm),
        in_specs=[pl.BlockSpec((tm, k1), lambda j, i: (i, 0)),
                  pl.BlockSpec((tm, k2), lambda j, i: (i, 0)),
                  _weight_spec(b1, k1, tn, layer, row_blocks[0], single=single),
                  _weight_spec(b2, k2, tn, layer, row_blocks[1], single=single)],
        out_specs=pl.BlockSpec((tm, tn), lambda j, i: (i, j)),
        out_shape=jax.ShapeDtypeStruct((m, n), out_dtype),
        scratch_shapes=[pltpu.VMEM((k1, tn), BF16), pltpu.VMEM((k2, tn), BF16)] if cast else [],
        compiler_params=_cparams(2, vmem),
        name=name,
    )(a1, a2, b1, b2)


def _top_k_gate(logits, bias):
    lane = lax.broadcasted_iota(jnp.int32, logits.shape, 1)
    scores = jax.nn.sigmoid(logits)
    sel = jnp.where(lane < N_EXPERTS, scores + bias, -jnp.inf)
    picked = jnp.zeros(logits.shape, jnp.bool_)
    for _ in range(TOP_K):
        best = jnp.max(sel, axis=1, keepdims=True)
        first = jnp.min(jnp.where(sel == best, lane, V7X_LANES), axis=1, keepdims=True)
        hit = lane == first
        picked = jnp.logical_or(picked, hit)
        sel = jnp.where(hit, -jnp.inf, sel)
    w = jnp.where(picked, scores, 0.0)
    return w / jnp.sum(w, axis=1, keepdims=True) * ROUTE_SCALE


def _add_ln_kernel(x_ref, r_ref, g_ref, b_ref, *rest, bf16_copy, route, project):
    rest = list(rest)
    if route:
        wr_ref, br_ref = rest.pop(0), rest.pop(0)
    if project:
        wp_ref = rest.pop(0)
    outs = rest
    v = DEEPNORM_ALPHA * x_ref[...] + r_ref[...].astype(F32)
    mu = jnp.mean(v, axis=-1, keepdims=True)
    c = v - mu
    var = jnp.mean(c * c, axis=-1, keepdims=True)
    y = c * lax.rsqrt(var + LN_EPS) * g_ref[...] + b_ref[...]
    outs[0][...] = y
    if bf16_copy:
        yb = y.astype(BF16)
        outs[1][...] = yb
        if route:
            logits = jnp.dot(yb, wr_ref[...], preferred_element_type=F32)
            outs[2][...] = _top_k_gate(logits, br_ref[...])
        if project:
            outs[-1][...] = lax.dot_general(yb, wp_ref[...], _NT, preferred_element_type=F32)


def add_layer_norm(x, r, g, b, *, tm, name, bf16_copy=True, router=None, proj_t=None):
    t, d = x.shape
    tm = min(tm, t)
    assert t % tm == 0 and ((router is None and proj_t is None) or bf16_copy) and (router is None or proj_t is None)
    vmem = 2 * (3 * _nbytes((tm, d), F32) + _nbytes((tm, d), BF16)) + 4 * _nbytes((tm, d), F32)
    row = lambda i: (i, 0)
    fixed = lambda i: (0, 0)
    in_specs = [pl.BlockSpec((tm, d), row), pl.BlockSpec((tm, d), row),
                pl.BlockSpec((1, d), fixed), pl.BlockSpec((1, d), fixed)]
    operands = [x, r, g.reshape(1, d), b.reshape(1, d)]
    out_specs = [pl.BlockSpec((tm, d), row)]
    out_shape = [jax.ShapeDtypeStruct((t, d), F32)]
    if bf16_copy:
        out_specs.append(pl.BlockSpec((tm, d), row))
        out_shape.append(jax.ShapeDtypeStruct((t, d), BF16))
    if router is not None:
        in_specs += [pl.BlockSpec((d, V7X_LANES), fixed), pl.BlockSpec((1, V7X_LANES), fixed)]
        operands += list(router)
        out_specs.append(pl.BlockSpec((tm, V7X_LANES), row))
        out_shape.append(jax.ShapeDtypeStruct((t, V7X_LANES), F32))
        vmem += 2 * _nbytes((d, V7X_LANES), BF16) + 10 * _nbytes((tm, V7X_LANES), F32)
    if proj_t is not None:
        in_specs.append(pl.BlockSpec((V7X_LANES, d), fixed))
        operands.append(proj_t)
        out_specs.append(pl.BlockSpec((tm, V7X_LANES), row))
        out_shape.append(jax.ShapeDtypeStruct((t, V7X_LANES), F32))
        vmem += 2 * _nbytes((d, V7X_LANES), BF16) + 2 * _nbytes((tm, V7X_LANES), F32)
    out = pl.pallas_call(
        functools.partial(_add_ln_kernel, bf16_copy=bf16_copy, route=router is not None, project=proj_t is not None),
        grid=(t // tm,),
        in_specs=in_specs,
        out_specs=out_specs,
        out_shape=out_shape,
        compiler_params=_cparams(1, vmem),
        name=name,
    )(*operands)
    return out[0], (out[1] if bf16_copy else None), (out[2] if (router is not None or proj_t is not None) else None)


_HALO = 16


def _conv3_silu_chunks(z_ref, zp_ref, zn_ref, w_ref, seq_len):
    tr = z_ref.shape[0]
    t0 = pl.program_id(0) * tr
    first = jnp.where(lax.rem(t0, seq_len) == 0, 0.0, zp_ref[_HALO - 1:_HALO, :].astype(F32))
    last = jnp.where(lax.rem(t0 + tr, seq_len) == 0, 0.0, zn_ref[0:1, :].astype(F32))
    row = lax.broadcasted_iota(jnp.int32, (CHUNK, z_ref.shape[1]), 0)
    w0, w1, w2 = w_ref[0:1, :], w_ref[1:2, :], w_ref[2:3, :]
    n = tr // CHUNK
    for c in range(n):
        lo = c * CHUNK
        x = z_ref[lo:lo + CHUNK, :].astype(F32)
        prev_row = first if c == 0 else z_ref[lo - 1:lo, :].astype(F32)
        next_row = last if c == n - 1 else z_ref[lo + CHUNK:lo + CHUNK + 1, :].astype(F32)
        x_prev = jnp.where(row == 0, prev_row, pltpu.roll(x, 1, axis=0))
        x_next = jnp.where(row == CHUNK - 1, next_row, pltpu.roll(x, CHUNK - 1, axis=0))
        y = w0 * x_prev + w1 * x + w2 * x_next
        yield c, 0.5 * y * (1.0 + jnp.tanh(0.5 * y))


def _conv_silu_kernel(q_ref, qp_ref, qn_ref, k_ref, kp_ref, kn_ref, wq_ref, wk_ref, oq_ref, okt_ref,
                      *, seq_len, k_scale):
    for c, q in _conv3_silu_chunks(q_ref, qp_ref, qn_ref, wq_ref, seq_len):
        oq_ref[c * CHUNK:(c + 1) * CHUNK, :] = q.astype(oq_ref.dtype)
    for c, k in _conv3_silu_chunks(k_ref, kp_ref, kn_ref, wk_ref, seq_len):
        okt_ref[c] = (k * k_scale).T.astype(okt_ref.dtype)


def conv_silu(z3, conv_w, *, n_slabs, seq_len, k_scale, tr, name):
    _, t, tc = z3.shape
    tr = min(tr, seq_len)
    assert seq_len % tr == 0 and tr % CHUNK == 0
    hb = tr // _HALO
    n_hb = t // _HALO
    vmem = 2 * (3 * _nbytes((tr, tc), BF16) + 4 * _nbytes((_HALO, tc), BF16) + _nbytes((tr, tc), BF16))
    vmem += 12 * _nbytes((tr, tc), F32)
    cur = lambda off: (lambda i, j: (j + off, i, 0))
    prev = lambda off: (lambda i, j: (j + off, jnp.maximum(i * hb - 1, 0), 0))
    nxt = lambda off: (lambda i, j: (j + off, jnp.minimum((i + 1) * hb, n_hb - 1), 0))
    return pl.pallas_call(
        functools.partial(_conv_silu_kernel, seq_len=seq_len, k_scale=k_scale),
        grid=(t // tr, n_slabs),
        in_specs=[pl.BlockSpec((None, tr, tc), cur(0)),
                  pl.BlockSpec((None, _HALO, tc), prev(0)),
                  pl.BlockSpec((None, _HALO, tc), nxt(0)),
                  pl.BlockSpec((None, tr, tc), cur(n_slabs)),
                  pl.BlockSpec((None, _HALO, tc), prev(n_slabs)),
                  pl.BlockSpec((None, _HALO, tc), nxt(n_slabs)),
                  pl.BlockSpec((3, tc), lambda i, j: (0, j)),
                  pl.BlockSpec((3, tc), lambda i, j: (0, j + n_slabs))],
        out_specs=[pl.BlockSpec((None, tr, tc), lambda i, j: (j, i, 0)),
                   pl.BlockSpec((tr // CHUNK, tc, CHUNK), lambda i, j: (i, j, 0))],
        out_shape=[jax.ShapeDtypeStruct((n_slabs, t, tc), BF16),
                   jax.ShapeDtypeStruct((t // CHUNK, n_slabs * tc, CHUNK), BF16)],
        compiler_params=_cparams(2, vmem),
        name=name,
    )(z3, z3, z3, z3, z3, z3, conv_w, conv_w)


def _chunk_scan(x, lane_in_chunk, *, suffix):
    n = x.shape[-1]
    k = 1
    while k < CHUNK:
        if suffix:
            shifted = pltpu.roll(x, n - k, axis=1)
            x = x + jnp.where(lane_in_chunk < CHUNK - k, shifted, 0.0)
        else:
            shifted = pltpu.roll(x, k, axis=1)
            x = x + jnp.where(lane_in_chunk >= k, shifted, 0.0)
        k *= 2
    return x


def _gate_prep_kernel(g_ref, bias_ref, o_ref):
    i_f = g_ref[0] + bias_ref[0]
    lf_f = jax.nn.log_sigmoid(g_ref[1] + bias_ref[1])
    i_b = g_ref[2] + bias_ref[2]
    lf_b = jax.nn.log_sigmoid(g_ref[3] + bias_ref[3])
    lane = lax.rem(lax.broadcasted_iota(jnp.int32, i_f.shape, 1), CHUNK)
    b_f = _chunk_scan(lf_f, lane, suffix=False)
    g_f = b_f + _chunk_scan(lf_f, lane, suffix=True) - lf_f
    b_b = _chunk_scan(lf_b, lane, suffix=True)
    g_b = b_b + _chunk_scan(lf_b, lane, suffix=False) - lf_b
    r_f = i_f - b_f
    r_b = i_b - b_b
    o_ref[0] = b_f
    o_ref[1] = r_f
    o_ref[2] = g_f + r_f
    o_ref[3] = g_f
    o_ref[4] = b_b
    o_ref[5] = r_b
    o_ref[6] = g_b + r_b
    o_ref[7] = g_b


def gate_prep(g4, bias, *, tl, name):
    _, h, t = g4.shape
    tl = min(tl, t)
    assert t % tl == 0 and tl % CHUNK == 0
    vmem = 2 * (_nbytes((4, h, tl), F32) + _nbytes((8, h, tl), F32)) + 16 * _nbytes((h, tl), F32)
    return pl.pallas_call(
        _gate_prep_kernel,
        grid=(t // tl,),
        in_specs=[pl.BlockSpec((4, h, tl), lambda i: (0, 0, i)),
                  pl.BlockSpec((4, h, 1), lambda i: (0, 0, 0))],
        out_specs=pl.BlockSpec((8, h, tl), lambda i: (0, 0, i)),
        out_shape=jax.ShapeDtypeStruct((8, h, t), F32),
        compiler_params=_cparams(1, vmem),
        name=name,
    )(g4, bias)


def _lanes(x, n):
    return x if n == V7X_LANES else jnp.concatenate([x] * (n // V7X_LANES), axis=1)


def _mlstm_chunk(q, kt, v_ext, b_row, r_row, a_row, st_ref, m11, mask, eye):
    dv = v_ext.shape[1] - V7X_LANES
    b_col = jnp.sum(jnp.where(eye, b_row, 0.0), axis=1, keepdims=True)
    b_wide = jnp.broadcast_to(b_col, (b_col.shape[0], V7X_LANES))
    n_src = r_row.shape[1]
    dmat = jnp.where(mask, _lanes(b_wide, n_src) + r_row, -jnp.inf)
    inter = b_wide + m11
    m_t = jnp.maximum(inter, jnp.max(dmat, axis=1, keepdims=True))
    w_intra = jnp.exp(dmat - _lanes(m_t, n_src))
    w_inter = jnp.exp(inter - m_t)
    s_qk = jnp.dot(q, kt, preferred_element_type=F32) * w_intra
    intra = jnp.dot(s_qk.astype(BF16), v_ext, preferred_element_type=F32)
    state = st_ref[...]
    carried = jnp.dot(q, state.astype(BF16), preferred_element_type=F32)
    num = intra[:, :dv] + _lanes(w_inter, dv) * carried[:, :dv]
    den = intra[:, dv:] + w_inter * carried[:, dv:]
    h = num / _lanes(jnp.maximum(jnp.abs(den), jnp.exp(-m_t)), dv)
    g11 = a_row[:, 0:1] - r_row[:, 0:1]
    m_new = jnp.maximum(g11 + m11, jnp.max(a_row, axis=1, keepdims=True))
    wa = jnp.exp(a_row - m_new)
    decay = jnp.exp(g11 + m11 - m_new)
    ktw = (kt.astype(F32) * wa).astype(BF16)
    st_ref[...] = decay * state + jnp.dot(ktw, v_ext, preferred_element_type=F32)
    return h, m_new


def _mlstm_kernel(q_ref, kt_ref, v_ref, o_ref, rows_ref, hg_ref, out_ref, acc_ref, stf_ref, stb_ref):
    nc = rows_ref.shape[0]
    L = CHUNK
    stf_ref[...] = jnp.zeros_like(stf_ref)
    stb_ref[...] = jnp.zeros_like(stb_ref)
    t_idx = lax.broadcasted_iota(jnp.int32, (L, L), 0)
    s_idx = lax.broadcasted_iota(jnp.int32, (L, L), 1)
    causal = s_idx <= t_idx
    anti = s_idx >= t_idx
    eye = s_idx == t_idx
    ones = jnp.ones((L, V7X_LANES), BF16)

    def directions(c, m_f, m_b):
        cb = nc - 1 - c
        sf = pl.ds(pl.multiple_of(c * L, L), L)
        sb = pl.ds(pl.multiple_of(cb * L, L), L)
        rows_f = rows_ref[c]
        rows_b = rows_ref[cb]
        h_f, m_f = _mlstm_chunk(q_ref[sf, :], kt_ref[c], jnp.concatenate([v_ref[sf, :], ones], axis=1),
                                rows_f[0:1, :], rows_f[1:2, :], rows_f[2:3, :], stf_ref, m_f, causal, eye)
        h_b, m_b = _mlstm_chunk(q_ref[sb, :], kt_ref[cb], jnp.concatenate([v_ref[sb, :], ones], axis=1),
                                rows_b[4:5, :], rows_b[5:6, :], rows_b[6:7, :], stb_ref, m_b, anti, eye)
        return sf, sb, h_f, h_b, m_f, m_b

    def first_half(c, carry):
        sf, sb, h_f, h_b, m_f, m_b = directions(c, *carry)
        acc_ref[sf, :] = h_f
        acc_ref[sb, :] = h_b
        return m_f, m_b

    def finish(sl, h_new):
        hh = (acc_ref[sl, :] + h_new) * jax.nn.sigmoid(o_ref[sl, :].astype(F32))
        mu = jnp.mean(hh, axis=1, keepdims=True)
        cen = hh - mu
        var = jnp.mean(cen * cen, axis=1, keepdims=True)
        out_ref[sl, :] = (cen * lax.rsqrt(var + LN_EPS) * hg_ref[...]).astype(out_ref.dtype)

    def second_half(c, carry):
        sf, sb, h_f, h_b, m_f, m_b = directions(c, *carry)
        finish(sf, h_f)
        finish(sb, h_b)
        return m_f, m_b

    zero = jnp.zeros((1, 1), F32)
    carry = lax.fori_loop(0, nc // 2, first_half, (zero, zero))
    lax.fori_loop(nc // 2, nc, second_half, carry)


def mlstm(q3, kt, z3, rows, head_g, *, batch, seq_len, name):
    t = q3.shape[1]
    nh, dh = MLSTM_HEADS, MLSTM_HEAD_DIM
    nc = seq_len // CHUNK
    assert nc % 2 == 0
    vmem = 2 * 4 * _nbytes((seq_len, dh), BF16)
    vmem += 2 * _nbytes((seq_len, dh), BF16) + _nbytes((seq_len, dh), F32) + 2 * _nbytes((nc, 8, CHUNK), F32)
    vmem += 4 * _nbytes((dh, dh + V7X_LANES), F32)
    return pl.pallas_call(
        _mlstm_kernel,
        grid=(batch, nh),
        in_specs=[pl.BlockSpec((None, seq_len, dh), lambda b, h: (h, b, 0)),
                  pl.BlockSpec((nc, dh, CHUNK), lambda b, h: (b, h, 0)),
                  pl.BlockSpec((None, seq_len, dh), lambda b, h: (2 * nh + h, b, 0)),
                  pl.BlockSpec((None, seq_len, dh), lambda b, h: (3 * nh + h, b, 0)),
                  pl.BlockSpec((None, nc, 8, CHUNK), lambda b, h: (h, b, 0, 0)),
                  pl.BlockSpec((None, 1, dh), lambda b, h: (h, 0, 0))],
        out_specs=pl.BlockSpec((seq_len, dh), lambda b, h: (b, h)),
        out_shape=jax.ShapeDtypeStruct((t, nh * dh), BF16),
        scratch_shapes=[pltpu.VMEM((seq_len, dh), F32),
                        pltpu.VMEM((dh, dh + V7X_LANES), F32), pltpu.VMEM((dh, dh + V7X_LANES), F32)],
        compiler_params=_cparams(2, vmem),
        name=name,
    )(q3, kt, z3, z3, rows, head_g.reshape(nh, 1, dh))


def _gelu_tanh(x):
    return 0.5 * x * (1.0 + jnp.tanh(math.sqrt(2.0 / math.pi) * (x + 0.044715 * (x * x * x))))


def _sgu_kernel(u_ref, v_ref, ng_ref, nb_ref, ws_ref, bs_ref, o_ref):
    n_chunks = u_ref.shape[0] // SGU_CHUNK
    ws = ws_ref[...]
    for c in range(n_chunks):
        sl = slice(c * SGU_CHUNK, (c + 1) * SGU_CHUNK)
        v = _gelu_tanh(v_ref[sl, :].astype(F32))
        mu = jnp.mean(v, axis=1, keepdims=True)
        cen = v - mu
        var = jnp.mean(cen * cen, axis=1, keepdims=True)
        vn = cen * lax.rsqrt(var + LN_EPS) * ng_ref[...] + nb_ref[...]
        sp = jnp.dot(ws, vn.astype(BF16), preferred_element_type=F32) + bs_ref[...]
        o_ref[sl, :] = (_gelu_tanh(u_ref[sl, :].astype(F32)) * sp).astype(o_ref.dtype)


def spatial_gating(z, norm_g, norm_b, w_s, b_s, *, u_col, v_col, tr, name):
    t = z.shape[0]
    g, dg = SGU_GROUPS, SGU_GROUP_DIM
    tr = min(tr, t)
    assert t % tr == 0 and tr % SGU_CHUNK == 0
    vmem = 2 * 3 * _nbytes((tr, dg), BF16) + 8 * _nbytes((SGU_CHUNK, dg), F32) * (tr // SGU_CHUNK)
    return pl.pallas_call(
        _sgu_kernel,
        grid=(g, t // tr),
        in_specs=[pl.BlockSpec((tr, dg), lambda j, i: (i, u_col + j)),
                  pl.BlockSpec((tr, dg), lambda j, i: (i, v_col + j)),
                  pl.BlockSpec((None, 1, dg), lambda j, i: (j, 0, 0)),
                  pl.BlockSpec((None, 1, dg), lambda j, i: (j, 0, 0)),
                  pl.BlockSpec((None, SGU_CHUNK, SGU_CHUNK), lambda j, i: (j, 0, 0)),
                  pl.BlockSpec((None, SGU_CHUNK, 1), lambda j, i: (j, 0, 0))],
        out_specs=pl.BlockSpec((tr, dg), lambda j, i: (i, j)),
        out_shape=jax.ShapeDtypeStruct((t, g * dg), BF16),
        compiler_params=_cparams(2, vmem),
        name=name,
    )(z, z, norm_g.reshape(g, 1, dg), norm_b.reshape(g, 1, dg), w_s.astype(BF16), b_s.reshape(g, SGU_CHUNK, 1))


def _swiglu(x, wg, wu):
    gate = jnp.dot(x, wg, preferred_element_type=F32)
    up = jnp.dot(x, wu, preferred_element_type=F32)
    return gate * jax.nn.sigmoid(gate) * up


def _routed_up_kernel(x_ref, comb_ref, wg_ref, wu_ref, h_ref, wgc_ref, wuc_ref):
    per = wg_ref.shape[0]
    fe = wg_ref.shape[2]
    j = pl.program_id(0)

    @pl.when(pl.program_id(1) == 0)
    def _():
        for e in range(per):
            wgc_ref[:, e * fe:(e + 1) * fe] = wg_ref[e]
            wuc_ref[:, e * fe:(e + 1) * fe] = wu_ref[e]

    h = _swiglu(x_ref[...], wgc_ref[...], wuc_ref[...])
    e_row = lax.broadcasted_iota(jnp.int32, (V7X_LANES, per * fe), 0)
    e_col = lax.broadcasted_iota(jnp.int32, (V7X_LANES, per * fe), 1) // fe + j * per
    expand = jnp.where(e_row == e_col, 1.0, 0.0).astype(BF16)
    comb = comb_ref[...]
    c_hi = comb.astype(BF16)
    c_lo = (comb - c_hi.astype(F32)).astype(BF16)
    scale = jnp.dot(jnp.concatenate([c_hi, c_lo], axis=1), jnp.concatenate([expand, expand], axis=0),
                    preferred_element_type=F32)
    h_ref[...] = (h * scale).astype(h_ref.dtype)


def routed_up(xb, comb, wg, wu, *, layer, tm, per, name):
    t, d = xb.shape
    _, n_e, _, fe = wg.shape
    tm = min(tm, t)
    assert t % tm == 0 and n_e % per == 0
    tf = per * fe
    vmem = 2 * (_nbytes((tm, d), BF16) + _nbytes((tm, V7X_LANES), F32) + 2 * _nbytes((d, tf), BF16))
    vmem += 2 * _nbytes((tm, tf), BF16) + 2 * _nbytes((d, tf), BF16) + 5 * _nbytes((tm, tf), F32)
    return pl.pallas_call(
        _routed_up_kernel,
        grid=(n_e // per, t // tm),
        in_specs=[pl.BlockSpec((tm, d), lambda j, i: (i, 0)),
                  pl.BlockSpec((tm, V7X_LANES), lambda j, i: (i, 0)),
                  pl.BlockSpec((None, per, d, fe), lambda j, i: (layer, j, 0, 0)),
                  pl.BlockSpec((None, per, d, fe), lambda j, i: (layer, j, 0, 0))],
        out_specs=pl.BlockSpec((tm, tf), lambda j, i: (i, j)),
        out_shape=jax.ShapeDtypeStruct((t, n_e * fe), BF16),
        scratch_shapes=[pltpu.VMEM((d, tf), BF16), pltpu.VMEM((d, tf), BF16)],
        compiler_params=_cparams(2, vmem),
        name=name,
    )(xb, comb, wg, wu)


def _shared_up_kernel(x_ref, wg_ref, wu_ref, h_ref):
    h_ref[...] = _swiglu(x_ref[...], wg_ref[...], wu_ref[...]).astype(h_ref.dtype)


def shared_up(xb, wg, wu, *, layer, tm, tf, name):
    t, d = xb.shape
    f = wg.shape[2]
    tm, tf = min(tm, t), min(tf, f)
    assert t % tm == 0 and f % tf == 0
    vmem = 2 * (_nbytes((tm, d), BF16) + 2 * _nbytes((d, tf), BF16) + _nbytes((tm, tf), BF16))
    vmem += 4 * _nbytes((tm, tf), F32)
    return pl.pallas_call(
        _shared_up_kernel,
        grid=(f // tf, t // tm),
        in_specs=[pl.BlockSpec((tm, d), lambda j, i: (i, 0)),
                  pl.BlockSpec((None, d, tf), lambda j, i: (layer, 0, j)),
                  pl.BlockSpec((None, d, tf), lambda j, i: (layer, 0, j))],
        out_specs=pl.BlockSpec((tm, tf), lambda j, i: (i, j)),
        out_shape=jax.ShapeDtypeStruct((t, f), BF16),
        compiler_params=_cparams(2, vmem),
        name=name,
    )(xb, wg, wu)


def kernel(x, w_in, b_gates, conv_qk, mlstm_norm_g, sgu_norm_g, sgu_norm_b, w_spatial, b_spatial, w_out,
           ln1_g, ln1_b, w_router, b_router, w_exp_gate, w_exp_up, w_exp_down, w_sh_gate, w_sh_up, w_sh_down,
           ln2_g, ln2_b):
    bsz, seq_len, d = x.shape
    depth = w_in.shape[0]
    t = bsz * seq_len
    nh, dh = MLSTM_HEADS, MLSTM_HEAD_DIM
    w_m = nh * dh
    w_s = SGU_GROUPS * SGU_GROUP_DIM
    off_g = 4 * w_m
    n_e, f_e = w_exp_gate.shape[1], w_exp_gate.shape[3]
    nc_total = t // CHUNK

    xf = x.reshape(t, d)
    xb = xf.astype(BF16)
    lane_pad = V7X_LANES - N_EXPERTS

    w_in_t = jnp.swapaxes(w_in, 1, 2)
    w_usv_t = w_in_t[:, off_g + N_GATES:, :]
    w_gate_t = jnp.pad(w_in_t[:, off_g:off_g + N_GATES, :], ((0, 0), (0, V7X_LANES - N_GATES), (0, 0))).astype(BF16)
    w_r = jnp.pad(w_router, ((0, 0), (0, 0), (0, lane_pad))).astype(BF16)
    b_r = jnp.pad(b_router, ((0, 0), (0, lane_pad))).reshape(depth, 1, V7X_LANES)
    wg_e, wu_e = w_exp_gate.astype(BF16), w_exp_up.astype(BF16)
    wd_e = w_exp_down.reshape(depth, n_e * f_e, d).astype(BF16)
    wg_s, wu_s, wd_s = w_sh_gate.astype(BF16), w_sh_up.astype(BF16), w_sh_down.astype(BF16)

    for l in range(depth):
        z_m = matmul(xb, w_in_t, tm=1024, tn=1024, out_dtype=BF16, n_cols=off_g, layer=l, rhs_t=True, slab=dh,
                     name=f"in_proj_m_{l}")
        z_s = matmul(xb, w_usv_t, tm=1024, tn=1024, out_dtype=BF16, layer=l, rhs_t=True, name=f"in_proj_s_{l}")
        if l == 0:
            g_raw = matmul(xb, w_gate_t, tm=1024, tn=V7X_LANES, out_dtype=F32, layer=0, rhs_t=True, name="gate_proj_0")
        g4 = g_raw[:, :N_GATES].T.reshape(4, nh, t)
        tables = gate_prep(g4, b_gates[l].reshape(4, nh, 1), tl=2048, name=f"gate_prep_{l}")
        rows = tables.transpose(1, 0, 2).reshape(nh, 8, nc_total, CHUNK).transpose(0, 2, 1, 3)
        q_c, kt_c = conv_silu(z_m, conv_qk[l], n_slabs=nh, seq_len=seq_len, k_scale=dh ** -0.5, tr=1024,
                              name=f"conv_silu_{l}")
        h_m = mlstm(q_c, kt_c, z_m, rows, mlstm_norm_g[l], batch=bsz, seq_len=seq_len, name=f"mlstm_{l}")
        h_s = spatial_gating(z_s, sgu_norm_g[l], sgu_norm_b[l], w_spatial[l], b_spatial[l],
                             u_col=0, v_col=w_s // SGU_GROUP_DIM, tr=1024, name=f"sgu_{l}")
        mix = matmul_cat(h_m, h_s, w_out, w_out, tm=1024, tn=1024, out_dtype=BF16, layer=l, row_blocks=(0, 1),
                         name=f"out_proj_{l}")
        x1, x1b, comb = add_layer_norm(xf, mix, ln1_g[l], ln1_b[l], tm=256, router=(w_r[l], b_r[l]),
                                       name=f"ln1_{l}")
        hid_r = routed_up(x1b, comb, wg_e, wu_e, layer=l, tm=1024, per=4, name=f"routed_up_{l}")
        hid_s = shared_up(x1b, wg_s, wu_s, layer=l, tm=1024, tf=512, name=f"shared_up_{l}")
        y = matmul_cat(hid_r, hid_s, wd_e, wd_s, tm=512, tn=1024, out_dtype=BF16, layer=l, single=True,
                       name=f"ffn_down_{l}")
        last = l + 1 == depth
        xf, xb, g_raw = add_layer_norm(x1, y, ln2_g[l], ln2_b[l], tm=256, bf16_copy=not last,
                                       proj_t=None if last else w_gate_t[l + 1], name=f"ln2_{l}")

    return xf.reshape(bsz, seq_len, d)
```
